```python
import math
import jax, jax.numpy as jnp
from jax import lax
import numpy as np

D_MODEL = 1024
BATCH = 8
SEQ = 4096
DEPTH = 2

HEAD_DIM = 64
N_DIFF_HEADS = 4
DIFF_V_DIM = 2 * HEAD_DIM
DIFF_QK_COLS = N_DIFF_HEADS * HEAD_DIM
DIFF_WIDTH = N_DIFF_HEADS * DIFF_V_DIM
N_SB_HEADS = 8
SB_WIDTH = N_SB_HEADS * HEAD_DIM
QBLOCK = 128
N_BUCKETS = 32
MAX_DISTANCE = 128
N_GROUPS = 4
EXPERTS_PER_GROUP = 8
N_EXPERTS = N_GROUPS * EXPERTS_PER_GROUP
TOP_K = 2
D_FF_EXPERT = 512
MOE_BLOCK = 128
EPS = 1e-6
NEG_INF = -1e30

COL_SIZES = (DIFF_QK_COLS, DIFF_QK_COLS, DIFF_QK_COLS, DIFF_QK_COLS, DIFF_WIDTH,
             SB_WIDTH, SB_WIDTH, SB_WIDTH, D_MODEL, D_MODEL)
IN_COLS = sum(COL_SIZES)

kernel_name = "hybrid_diffattn_stickbreak_hiermoe"


def _split_points():
    pts, acc = [], 0
    for c in COL_SIZES[:-1]:
        acc += c
        pts.append(acc)
    return pts


def rmsnorm(x, g):
    x32 = x.astype(jnp.float32)
    y = x32 * lax.rsqrt(jnp.mean(x32 * x32, axis=-1, keepdims=True) + EPS)
    return (y * g.astype(jnp.float32)).astype(x.dtype)


def t5_bucket(n):
    max_exact = N_BUCKETS // 2
    nf = jnp.maximum(n, max_exact).astype(jnp.float32)
    large = max_exact + (jnp.log(nf / max_exact) / math.log(MAX_DISTANCE / max_exact)
                         * (N_BUCKETS - max_exact)).astype(jnp.int32)
    large = jnp.minimum(large, N_BUCKETS - 1)
    return jnp.where(n < max_exact, n, large)


def diff_attention(q1, q2, k1, k2, v, rel_bias, lam):
    b, h, s, _ = q1.shape
    nblk = s // QBLOCK
    kpos = jnp.arange(s)
    scale = HEAD_DIM ** -0.5

    def block(i):
        start = i * QBLOCK
        qpos = start + jnp.arange(QBLOCK)
        dist = qpos[:, None] - kpos[None, :]
        mask = dist >= 0
        bias = jnp.transpose(rel_bias[t5_bucket(jnp.maximum(dist, 0))], (2, 0, 1)).astype(jnp.float32)

        def probs(q, k):
            qb = lax.dynamic_slice_in_dim(q, start, QBLOCK, axis=2)
            sc = jnp.einsum('bhqd,bhkd->bhqk', qb, k).astype(jnp.float32) * scale + bias
            return jax.nn.softmax(jnp.where(mask, sc, NEG_INF), axis=-1)

        w = probs(q1, k1) - lam * probs(q2, k2)
        return jnp.einsum('bhqk,bhkv->bhqv', w.astype(v.dtype), v)

    o = lax.map(block, jnp.arange(nblk))
    return jnp.transpose(o, (1, 2, 0, 3, 4)).reshape(b, h, s, v.shape[-1])


def stick_breaking_attention(q, k, v):
    b, h, s, d = q.shape
    nblk = s // QBLOCK
    kpos = jnp.arange(s)
    scale = HEAD_DIM ** -0.5

    def block(i):
        start = i * QBLOCK
        qpos = start + jnp.arange(QBLOCK)
        mask = kpos[None, :] < qpos[:, None]
        qb = lax.dynamic_slice_in_dim(q, start, QBLOCK, axis=2)
        z = jnp.einsum('bhqd,bhkd->bhqk', qb, k).astype(jnp.float32) * scale
        log_1m = jnp.where(mask, jax.nn.log_sigmoid(-z), 0.0)
        suffix = lax.cumsum(log_1m, axis=3, reverse=True) - log_1m
        a = jnp.where(mask, jnp.exp(jax.nn.log_sigmoid(z) + suffix), 0.0)
        return jnp.einsum('bhqk,bhkd->bhqd', a.astype(v.dtype), v)

    o = lax.map(block, jnp.arange(nblk))
    return jnp.transpose(o, (1, 2, 0, 3, 4)).reshape(b, h, s, d)


def hier_moe(h, w_group, w_router, w_gate, w_up, w_down):
    b, s, d = h.shape
    t = b * s
    xt = h.reshape(t, d)
    gl = (xt @ w_group).astype(jnp.float32)
    gp = jax.nn.softmax(gl, axis=-1)
    g = jnp.argmax(gl, axis=-1).astype(jnp.int32)
    pg = jnp.take_along_axis(gp, g[:, None], axis=-1)
    el = (xt @ w_router).astype(jnp.float32).reshape(t, N_GROUPS, EXPERTS_PER_GROUP)
    el = jnp.take_along_axis(el, g[:, None, None], axis=1)[:, 0]
    topv, topi = lax.top_k(jax.nn.softmax(el, axis=-1), TOP_K)
    gate = pg * topv / jnp.sum(topv, axis=-1, keepdims=True)
    eid = g[:, None] * EXPERTS_PER_GROUP + topi.astype(jnp.int32)

    flat_e = eid.reshape(-1)
    flat_tok = jnp.broadcast_to(jnp.arange(t, dtype=jnp.int32)[:, None], (t, TOP_K)).reshape(-1)
    flat_w = gate.reshape(-1)
    n = t * TOP_K
    p = ((n + MOE_BLOCK - 1) // MOE_BLOCK) * MOE_BLOCK + N_EXPERTS * MOE_BLOCK
    nblk = p // MOE_BLOCK

    order = jnp.argsort(flat_e, stable=True)
    se = flat_e[order]
    counts = jnp.bincount(flat_e, length=N_EXPERTS).astype(jnp.int32)
    starts = jnp.cumsum(counts) - counts
    padded = ((counts + MOE_BLOCK - 1) // MOE_BLOCK) * MOE_BLOCK
    pend = jnp.cumsum(padded)
    pstarts = pend - padded
    dest = pstarts[se] + (jnp.arange(n, dtype=jnp.int32) - starts[se])
    row_tok = jnp.full((p,), t, jnp.int32).at[dest].set(flat_tok[order])
    row_w = jnp.zeros((p,), jnp.float32).at[dest].set(flat_w[order])
    blk_e = jnp.clip(jnp.searchsorted(pend, jnp.arange(nblk, dtype=jnp.int32) * MOE_BLOCK,
                                      side='right'), 0, N_EXPERTS - 1)

    xpad = jnp.concatenate([xt, jnp.zeros((1, d), xt.dtype)], axis=0)

    def expert_block(args):
        rows, w, e = args
        xb = xpad[rows]
        a = jax.nn.silu(xb @ w_gate[e]) * (xb @ w_up[e])
        return (a @ w_down[e]) * w[:, None].astype(xb.dtype)

    yb = lax.map(expert_block, (row_tok.reshape(nblk, MOE_BLOCK),
                                row_w.reshape(nblk, MOE_BLOCK), blk_e))
    out = jnp.zeros((t + 1, d), h.dtype).at[row_tok].add(yb.reshape(p, d).astype(h.dtype))[:t]
    return out.reshape(b, s, d)


def setup_inputs(seed: int = 0) -> dict:
    key = jax.random.key(seed)
    ks = jax.random.split(key, 20)
    f32 = jnp.float32
    nrm = lambda k, shape, scale: jax.random.normal(k, shape, f32) * scale
    gain = lambda k, shape: 1.0 + 0.02 * jax.random.normal(k, shape, f32)
    return {
        "x": jax.random.normal(ks[0], (BATCH, SEQ, D_MODEL), f32),
        "rel_bias": nrm(ks[1], (N_BUCKETS, N_DIFF_HEADS), 0.5),
        "ln1_g": gain(ks[2], (DEPTH, D_MODEL)),
        "w_in": nrm(ks[3], (DEPTH, D_MODEL, IN_COLS), D_MODEL ** -0.5),
        "qnorm_g": gain(ks[4], (DEPTH, HEAD_DIM)),
        "knorm_g": gain(ks[5], (DEPTH, HEAD_DIM)),
        "lambda_q1": nrm(ks[6], (DEPTH, HEAD_DIM), 0.1),
        "lambda_k1": nrm(ks[7], (DEPTH, HEAD_DIM), 0.1),
        "lambda_q2": nrm(ks[8], (DEPTH, HEAD_DIM), 0.1),
        "lambda_k2": nrm(ks[9], (DEPTH, HEAD_DIM), 0.1),
        "subln_g": gain(ks[10], (DEPTH, DIFF_V_DIM)),
        "w_branch_diff": nrm(ks[11], (DEPTH, DIFF_WIDTH, D_MODEL), DIFF_WIDTH ** -0.5),
        "w_branch_sb": nrm(ks[12], (DEPTH, SB_WIDTH, D_MODEL), SB_WIDTH ** -0.5),
        "w_out": nrm(ks[13], (DEPTH, D_MODEL, D_MODEL), D_MODEL ** -0.5),
        "ln2_g": gain(ks[14], (DEPTH, D_MODEL)),
        "w_group": nrm(ks[15], (DEPTH, D_MODEL, N_GROUPS), D_MODEL ** -0.5),
        "w_router": nrm(ks[16], (DEPTH, D_MODEL, N_EXPERTS), D_MODEL ** -0.5),
        "w_gate": nrm(ks[17], (DEPTH, N_EXPERTS, D_MODEL, D_FF_EXPERT), D_MODEL ** -0.5),
        "w_up": nrm(ks[18], (DEPTH, N_EXPERTS, D_MODEL, D_FF_EXPERT), D_MODEL ** -0.5),
        "w_down": nrm(ks[19], (DEPTH, N_EXPERTS, D_FF_EXPERT, D_MODEL), D_FF_EXPERT ** -0.5),
    }


def reference(x, rel_bias, ln1_g, w_in, qnorm_g, knorm_g, lambda_q1, lambda_k1, lambda_q2,
              lambda_k2, subln_g, w_branch_diff, w_branch_sb, w_out, ln2_g, w_group, w_router,
              w_gate, w_up, w_down):
    b, s, _ = x.shape
    split_pts = _split_points()

    def heads(t, nh):
        return t.reshape(b, s, nh, -1).transpose(0, 2, 1, 3)

    def merge_heads(t):
        return t.transpose(0, 2, 1, 3).reshape(b, s, -1)

    for l in range(DEPTH):
        h = rmsnorm(x, ln1_g[l])
        proj = h @ w_in[l]
        q1, q2, k1, k2, va, qs, kss, vs, ga, gb = jnp.split(proj, split_pts, axis=-1)

        q1 = rmsnorm(heads(q1, N_DIFF_HEADS), qnorm_g[l])
        q2 = rmsnorm(heads(q2, N_DIFF_HEADS), qnorm_g[l])
        k1 = rmsnorm(heads(k1, N_DIFF_HEADS), knorm_g[l])
        k2 = rmsnorm(heads(k2, N_DIFF_HEADS), knorm_g[l])
        lam_init = 0.8 - 0.6 * math.exp(-0.3 * l)
        lam = (jnp.exp(jnp.sum(lambda_q1[l].astype(jnp.float32) * lambda_k1[l].astype(jnp.float32)))
               - jnp.exp(jnp.sum(lambda_q2[l].astype(jnp.float32) * lambda_k2[l].astype(jnp.float32)))
               + lam_init)
        oa = diff_attention(q1, q2, k1, k2, heads(va, N_DIFF_HEADS), rel_bias, lam)
        oa = merge_heads(rmsnorm(oa, subln_g[l]) * (1.0 - lam_init))

        ob = merge_heads(stick_breaking_attention(heads(qs, N_SB_HEADS), heads(kss, N_SB_HEADS),
                                                  heads(vs, N_SB_HEADS)))

        mixed = (jax.nn.sigmoid(ga) * (oa @ w_branch_diff[l])
                 + jax.nn.sigmoid(gb) * (ob @ w_branch_sb[l]))
        x = x + mixed @ w_out[l]

        x = x + hier_moe(rmsnorm(x, ln2_g[l]), w_group[l], w_router[l], w_gate[l], w_up[l], w_down[l])
    return x
```

```python
import functools
import math

import jax
import jax.numpy as jnp
from jax import lax
from jax.experimental import pallas as pl
from jax.experimental.pallas import tpu as pltpu

F32 = jnp.float32
BF16 = jnp.bfloat16
I32 = jnp.int32

HEAD_DIM = 64
N_DIFF_HEADS = 4
DIFF_V_DIM = 2 * HEAD_DIM
N_SB_HEADS = 8
N_BUCKETS = 32
MAX_DISTANCE = 128
N_GROUPS = 4
EXPERTS_PER_GROUP = 8
N_EXPERTS = N_GROUPS * EXPERTS_PER_GROUP
EPS = 1e-6
NEG_INF = -1e30

V7X_LANES = 128
V7X_VMEM_BYTES = 64 * 1024 * 1024
VMEM_LIMIT = 56 * 1024 * 1024

F32_EXP_ZERO_BELOW = -104.0

QK_COLS = 4 * N_DIFF_HEADS * HEAD_DIM
VA_COLS = N_DIFF_HEADS * DIFF_V_DIM
SB_COLS = N_SB_HEADS * HEAD_DIM
GATE_OFF = QK_COLS + VA_COLS + 3 * SB_COLS

HEADS_PER_STACK = 4
STACK_COLS = HEADS_PER_STACK * HEAD_DIM


def _cparams(semantics, vmem=VMEM_LIMIT):
    return pltpu.CompilerParams(dimension_semantics=semantics, vmem_limit_bytes=vmem)


def _nt_dot(a, b):
    return lax.dot_general(a, b, (((1,), (1,)), ((), ())), preferred_element_type=F32)


def _split_bf16(x):
    hi = x.astype(BF16)
    lo = (x - hi.astype(F32)).astype(BF16)
    return hi, lo


def _in_proj_kernel(x_ref, g_ref, w_ref, qkg_ref, seg_ref, o_ref, *, d_model, in_cols):
    x = x_ref[...]
    ms = jnp.mean(x * x, axis=-1, keepdims=True)
    h = ((x * lax.rsqrt(ms + EPS)) * g_ref[...]).astype(BF16)
    seg = seg_ref[...]
    col = 0
    while col < in_cols:
        width = STACK_COLS if col < QK_COLS else 2 * STACK_COLS
        acc = jnp.dot(h, w_ref[:, col:col + width], preferred_element_type=F32)
        if col < QK_COLS:
            hi, lo = _split_bf16(acc * acc)
            msq = (jnp.dot(hi, seg, preferred_element_type=F32)
                   + jnp.dot(lo, seg, preferred_element_type=F32))
            acc = (acc * lax.rsqrt(msq + EPS)) * qkg_ref[:, col:col + width]
        elif QK_COLS + VA_COLS <= col < QK_COLS + VA_COLS + SB_COLS:
            acc = acc * (HEAD_DIM ** -0.5)
        elif col >= GATE_OFF:
            acc = jax.nn.sigmoid(acc)
        o_ref[:, col:col + width] = acc.astype(BF16)
        col += width


def _in_proj(x2, ln_g, w_bf16, qk_gain, seg, *, tm):
    t, d_model = x2.shape
    in_cols = w_bf16.shape[1]
    kern = functools.partial(_in_proj_kernel, d_model=d_model, in_cols=in_cols)
    return pl.pallas_call(
        kern,
        out_shape=jax.ShapeDtypeStruct((t, in_cols), BF16),
        grid=(t // tm,),
        in_specs=[
            pl.BlockSpec((tm, d_model), lambda i: (i, 0)),
            pl.BlockSpec((1, d_model), lambda i: (0, 0)),
            pl.BlockSpec((d_model, in_cols), lambda i: (0, 0)),
            pl.BlockSpec((1, QK_COLS), lambda i: (0, 0)),
            pl.BlockSpec((STACK_COLS, STACK_COLS), lambda i: (0, 0)),
        ],
        out_specs=pl.BlockSpec((tm, in_cols), lambda i: (i, 0)),
        compiler_params=_cparams(("arbitrary",)),
        name="in_proj",
    )(x2, ln_g, w_bf16, qk_gain, seg)


def _stack_heads(q, tq):
    qf = q.astype(F32)
    lane_head = lax.broadcasted_iota(I32, (tq, STACK_COLS), 1) // HEAD_DIM
    return jnp.concatenate(
        [jnp.where(lane_head == h, qf, 0.0) for h in range(HEADS_PER_STACK)], axis=0).astype(BF16)


def _t5_bucket(n):
    max_exact = N_BUCKETS // 2
    nf = jnp.maximum(n, max_exact).astype(F32)
    large = max_exact + (jnp.log(nf / max_exact) / math.log(MAX_DISTANCE / max_exact)
                         * (N_BUCKETS - max_exact)).astype(I32)
    large = jnp.minimum(large, N_BUCKETS - 1)
    return jnp.where(n < max_exact, n, large)


def _bias_tiles(rel_bias, tq):
    assert tq + 1 >= MAX_DISTANCE, "far tiles must lie entirely in the last bucket"
    dist = jnp.arange(tq, dtype=I32)[:, None] - jnp.arange(tq, dtype=I32)[None, :]
    rb = rel_bias.astype(F32)

    def tile(d):
        return jnp.transpose(rb[_t5_bucket(jnp.maximum(d, 0))], (2, 0, 1))

    diag = jnp.where(dist[None] >= 0, tile(dist), NEG_INF)
    near = tile(dist + tq)
    far = tile(jnp.full_like(dist, 2 * tq))
    return jnp.stack([diag, near, far]).reshape(3, N_DIFF_HEADS * tq, tq)


def _diff_attn_kernel(lam_ref, subg_ref, bias_ref, q_ref, k_ref, v_ref, o_ref,
                      m1_sc, l1_sc, a1_sc, m2_sc, l2_sc, a2_sc, *, tq, lam_init):
    i = pl.program_id(1)
    q = q_ref[0]
    q1s = _stack_heads(q[:, :STACK_COLS], tq)
    q2s = _stack_heads(q[:, STACK_COLS:], tq)
    for m_sc, l_sc, a_sc in ((m1_sc, l1_sc, a1_sc), (m2_sc, l2_sc, a2_sc)):
        m_sc[...] = jnp.full(m_sc.shape, NEG_INF, F32)
        l_sc[...] = jnp.zeros(l_sc.shape, F32)
        a_sc[...] = jnp.zeros(a_sc.shape, F32)

    def update(qs, kb, vb, bias, m_sc, l_sc, a_sc):
        s = _nt_dot(qs, kb) + bias
        m_old = m_sc[...]
        m_new = jnp.maximum(m_old, jnp.max(s, axis=1, keepdims=True))
        alpha = jnp.exp(m_old - m_new)
        p = jnp.exp(s - m_new)
        l_sc[...] = alpha * l_sc[...] + jnp.sum(p, axis=1, keepdims=True)
        pb = p.astype(BF16)
        pv = jnp.concatenate(
            [jnp.dot(pb[h * tq:(h + 1) * tq], vb[:, h * DIFF_V_DIM:(h + 1) * DIFF_V_DIM],
                     preferred_element_type=F32) for h in range(N_DIFF_HEADS)], axis=0)
        a_sc[...] = alpha * a_sc[...] + pv
        m_sc[...] = m_new

    def body(j, carry):
        start = pl.multiple_of(j * tq, tq)
        kb = k_ref[0, pl.ds(start, tq), :]
        vb = v_ref[0, pl.ds(start, tq), :]
        bias = bias_ref[jnp.minimum(i - j, 2)]
        update(q1s, kb[:, :STACK_COLS], vb, bias, m1_sc, l1_sc, a1_sc)
        update(q2s, kb[:, STACK_COLS:], vb, bias, m2_sc, l2_sc, a2_sc)
        return carry

    lax.fori_loop(0, i + 1, body, 0)

    lamv = lam_ref[...]
    lam = (jnp.exp(jnp.sum(lamv[0:1] * lamv[1:2], axis=1, keepdims=True))
           - jnp.exp(jnp.sum(lamv[2:3] * lamv[3:4], axis=1, keepdims=True)) + lam_init)
    o = a1_sc[...] / l1_sc[...] - lam * (a2_sc[...] / l2_sc[...])
    ms = jnp.mean(o * o, axis=-1, keepdims=True)
    o = ((o * lax.rsqrt(ms + EPS)) * subg_ref[...]) * (1.0 - lam_init)
    for h in range(N_DIFF_HEADS):
        o_ref[0, :, h * DIFF_V_DIM:(h + 1) * DIFF_V_DIM] = o[h * tq:(h + 1) * tq].astype(BF16)


def _diff_attn(proj3, lamv, subln_g, bias, *, tq, lam_init):
    b, s, _ = proj3.shape
    m = N_DIFF_HEADS * tq
    kern = functools.partial(_diff_attn_kernel, tq=tq, lam_init=lam_init)
    kv_blk = 2 * STACK_COLS
    return pl.pallas_call(
        kern,
        out_shape=jax.ShapeDtypeStruct((b, s, VA_COLS), BF16),
        grid=(b, s // tq),
        in_specs=[
            pl.BlockSpec((4, HEAD_DIM), lambda bi, i: (0, 0)),
            pl.BlockSpec((1, DIFF_V_DIM), lambda bi, i: (0, 0)),
            pl.BlockSpec((3, m, tq), lambda bi, i: (0, 0, 0)),
            pl.BlockSpec((1, tq, kv_blk), lambda bi, i: (bi, i, 0)),
            pl.BlockSpec((1, s, kv_blk), lambda bi, i: (bi, 0, 1)),
            pl.BlockSpec((1, s, VA_COLS), lambda bi, i: (bi, 0, QK_COLS // VA_COLS)),
        ],
        out_specs=pl.BlockSpec((1, tq, VA_COLS), lambda bi, i: (bi, i, 0)),
        scratch_shapes=[
            pltpu.VMEM((m, 1), F32), pltpu.VMEM((m, 1), F32), pltpu.VMEM((m, DIFF_V_DIM), F32),
            pltpu.VMEM((m, 1), F32), pltpu.VMEM((m, 1), F32), pltpu.VMEM((m, DIFF_V_DIM), F32),
        ],
        compiler_params=_cparams(("arbitrary", "arbitrary")),
        name="diff_attn",
    )(lamv, subln_g, bias, proj3, proj3, proj3)


def _sb_attn_kernel(tri_ref, q_ref, k_ref, v_ref, o_ref, acc_sc, c_sc, *, tq):
    i = pl.program_id(2)
    m = HEADS_PER_STACK * tq
    qs = _stack_heads(q_ref[0], tq)
    acc_sc[...] = jnp.zeros(acc_sc.shape, F32)
    c_sc[...] = jnp.zeros(c_sc.shape, F32)

    def block(j, on_diagonal):
        start = pl.multiple_of(j * tq, tq)
        kb = k_ref[0, pl.ds(start, tq), :]
        vb = v_ref[0, pl.ds(start, tq), :]
        z = _nt_dot(qs, kb)
        log_1m = -(jnp.maximum(z, 0.0) + jnp.log1p(jnp.exp(-jnp.abs(z))))
        log_sig = z + log_1m
        if on_diagonal:
            qrow = lax.broadcasted_iota(I32, (m, tq), 0) % tq
            kcol = lax.broadcasted_iota(I32, (m, tq), 1)
            mask = kcol < qrow
            log_1m = jnp.where(mask, log_1m, 0.0)
        hi, lo = _split_bf16(log_1m)
        tri = tri_ref[...]
        suffix = (jnp.dot(hi, tri, preferred_element_type=F32)
                  + jnp.dot(lo, tri, preferred_element_type=F32)) + c_sc[...]
        a = jnp.exp(log_sig + suffix)
        if on_diagonal:
            a = jnp.where(mask, a, 0.0)
        acc_sc[...] += jnp.dot(a.astype(BF16), vb, preferred_element_type=F32)
        c_sc[...] += jnp.sum(log_1m, axis=1, keepdims=True)

    block(i, True)

    def cond(j):
        return jnp.logical_and(j >= 0, jnp.max(c_sc[...]) > F32_EXP_ZERO_BELOW)

    def body(j):
        block(j, False)
        return j - 1

    lax.while_loop(cond, body, i - 1)

    acc = acc_sc[...]
    lane_head = lax.broadcasted_iota(I32, (tq, STACK_COLS), 1) // HEAD_DIM
    out = jnp.zeros((tq, STACK_COLS), F32)
    for h in range(HEADS_PER_STACK):
        out = out + jnp.where(lane_head == h, acc[h * tq:(h + 1) * tq], 0.0)
    o_ref[0] = out.astype(BF16)


def _sb_attn(proj3, tri, *, tq):
    b, s, _ = proj3.shape
    n_stacks = SB_COLS // STACK_COLS
    m = HEADS_PER_STACK * tq
    q_off = (QK_COLS + VA_COLS) // STACK_COLS
    k_off = q_off + n_stacks
    v_off = k_off + n_stacks
    kern = functools.partial(_sb_attn_kernel, tq=tq)
    return pl.pallas_call(
        kern,
        out_shape=jax.ShapeDtypeStruct((b, s, SB_COLS), BF16),
        grid=(b, n_stacks, s // tq),
        in_specs=[
            pl.BlockSpec((tq, tq), lambda bi, g, i: (0, 0)),
            pl.BlockSpec((1, tq, STACK_COLS), lambda bi, g, i: (bi, i, q_off + g)),
            pl.BlockSpec((1, s, STACK_COLS), lambda bi, g, i: (bi, 0, k_off + g)),
            pl.BlockSpec((1, s, STACK_COLS), lambda bi, g, i: (bi, 0, v_off + g)),
        ],
        out_specs=pl.BlockSpec((1, tq, STACK_COLS), lambda bi, g, i: (bi, i, g)),
        scratch_shapes=[pltpu.VMEM((m, STACK_COLS), F32), pltpu.VMEM((m, 1), F32)],
        compiler_params=_cparams(("arbitrary", "arbitrary", "arbitrary")),
        name="sb_attn",
    )(tri, proj3, proj3, proj3)


ROUTER_ROWS = 8 + N_EXPERTS


def _post_attn_kernel(x_ref, oa_ref, ob_ref, ga_ref, gb_ref, wbd_ref, wbs_ref, wo_ref, g_ref,
                      wrh_ref, wrl_ref, tri_ref,
                      x1_ref, xn_ref, route_ref, gates_ref, counts_ref, cnt_sc, *, tm, d_model):
    @pl.when(pl.program_id(0) == 0)
    def _():
        cnt_sc[...] = jnp.zeros(cnt_sc.shape, F32)

    ga = ga_ref[...].astype(F32)
    gb = gb_ref[...].astype(F32)
    mixed = (ga * jnp.dot(oa_ref[...], wbd_ref[...], preferred_element_type=F32)
             + gb * jnp.dot(ob_ref[...], wbs_ref[...], preferred_element_type=F32))
    x1 = x_ref[...] + jnp.dot(mixed.astype(BF16), wo_ref[...], preferred_element_type=F32)
    x1_ref[...] = x1
    ms = jnp.mean(x1 * x1, axis=-1, keepdims=True)
    xn = (x1 * lax.rsqrt(ms + EPS)) * g_ref[...]
    xn_ref[...] = xn

    xh, xl = _split_bf16(xn)
    wrh = wrh_ref[...]
    logits = _nt_dot(wrh, xh) + _nt_dot(wrh, xl) + _nt_dot(wrl_ref[...], xh)

    gl = [logits[r:r + 1] for r in range(N_GROUPS)]
    gmax = functools.reduce(jnp.maximum, gl)
    grp = jnp.full((1, tm), N_GROUPS - 1, I32)
    for r in range(N_GROUPS - 2, -1, -1):
        grp = jnp.where(gl[r] == gmax, r, grp)
    pg = 1.0 / functools.reduce(lambda a, b: a + b, [jnp.exp(v - gmax) for v in gl])

    el = logits[8:8 + EXPERTS_PER_GROUP]
    for r in range(1, N_GROUPS):
        lo = 8 + r * EXPERTS_PER_GROUP
        el = jnp.where(grp == r, logits[lo:lo + EXPERTS_PER_GROUP], el)
    ex = jnp.exp(el - jnp.max(el, axis=0, keepdims=True))
    prob = ex / jnp.sum(ex, axis=0, keepdims=True)
    sub = lax.broadcasted_iota(I32, (EXPERTS_PER_GROUP, tm), 0).astype(F32)
    none = float(EXPERTS_PER_GROUP)
    v1 = jnp.max(prob, axis=0, keepdims=True)
    i1 = jnp.min(jnp.where(prob == v1, sub, none), axis=0, keepdims=True)
    rest = jnp.where(sub == i1, -1.0, prob)
    v2 = jnp.max(rest, axis=0, keepdims=True)
    i2 = jnp.min(jnp.where(rest == v2, sub, none), axis=0, keepdims=True)
    denom = v1 + v2
    gate1 = pg * v1 / denom
    gate2 = pg * v2 / denom
    e1 = grp * EXPERTS_PER_GROUP + i1.astype(I32)
    e2 = grp * EXPERTS_PER_GROUP + i2.astype(I32)

    eio = lax.broadcasted_iota(I32, (N_EXPERTS, tm), 0)
    hit1 = eio == e1
    hit2 = eio == e2
    onehot = jnp.where(hit1, 1.0, 0.0) + jnp.where(hit2, 1.0, 0.0)
    prefix = jnp.dot(onehot.astype(BF16), tri_ref[...], preferred_element_type=F32) + cnt_sc[...]
    r1 = jnp.sum(jnp.where(hit1, prefix, 0.0), axis=0, keepdims=True)
    r2 = jnp.sum(jnp.where(hit2, prefix, 0.0), axis=0, keepdims=True)
    cnt = cnt_sc[...] + jnp.sum(onehot, axis=1, keepdims=True)
    cnt_sc[...] = cnt
    counts_ref[...] = jnp.broadcast_to(cnt, counts_ref.shape).astype(I32)

    row = lax.broadcasted_iota(I32, (8, tm), 0)
    route_ref[...] = jnp.where(row == 0, e1, jnp.where(row == 1, e2, jnp.where(
        row == 2, r1.astype(I32), jnp.where(row == 3, r2.astype(I32), 0))))
    gates_ref[...] = jnp.where(row == 0, gate1, jnp.where(row == 1, gate2, 0.0))


def _post_attn(x2, oa, ob, proj, wbd, wbs, wo, ln_g, wrh, wrl, tri, *, tm):
    t, d_model = x2.shape
    kern = functools.partial(_post_attn_kernel, tm=tm, d_model=d_model)
    gate_blk = GATE_OFF // d_model
    assert gate_blk * d_model == GATE_OFF
    const = lambda i: (0, 0)
    return pl.pallas_call(
        kern,
        out_shape=(
            jax.ShapeDtypeStruct((t, d_model), F32),
            jax.ShapeDtypeStruct((t, d_model), F32),
            jax.ShapeDtypeStruct((8, t), I32),
            jax.ShapeDtypeStruct((8, t), F32),
            jax.ShapeDtypeStruct((N_EXPERTS, V7X_LANES), I32),
        ),
        grid=(t // tm,),
        in_specs=[
            pl.BlockSpec((tm, d_model), lambda i: (i, 0)),
            pl.BlockSpec((tm, VA_COLS), lambda i: (i, 0)),
            pl.BlockSpec((tm, SB_COLS), lambda i: (i, 0)),
            pl.BlockSpec((tm, d_model), lambda i: (i, gate_blk)),
            pl.BlockSpec((tm, d_model), lambda i: (i, gate_blk + 1)),
            pl.BlockSpec((VA_COLS, d_model), const),
            pl.BlockSpec((SB_COLS, d_model), const),
            pl.BlockSpec((d_model, d_model), const),
            pl.BlockSpec((1, d_model), const),
            pl.BlockSpec((ROUTER_ROWS, d_model), const),
            pl.BlockSpec((ROUTER_ROWS, d_model), const),
            pl.BlockSpec((tm, tm), const),
        ],
        out_specs=(
            pl.BlockSpec((tm, d_model), lambda i: (i, 0)),
            pl.BlockSpec((tm, d_model), lambda i: (i, 0)),
            pl.BlockSpec((8, tm), lambda i: (0, i)),
            pl.BlockSpec((8, tm), lambda i: (0, i)),
            pl.BlockSpec((N_EXPERTS, V7X_LANES), const),
        ),
        scratch_shapes=[pltpu.VMEM((N_EXPERTS, 1), F32)],
        compiler_params=_cparams(("arbitrary",)),
        name="post_attn",
    )(x2, oa, ob, proj, proj, wbd, wbs, wo, ln_g, wrh, wrl, tri)


ROW_UNROLL = 8


def _row_copy(src, src_row, dst, dst_row, sem):
    return pltpu.make_async_copy(src.at[pl.ds(src_row, 1), :], dst.at[pl.ds(dst_row, 1), :], sem)


def _dispatch_kernel(pstart_ref, lastblk_ref, nblk_ref, route_ref, xn_ref, xs_ref, zero_sc, sem, zsem,
                     *, tm, tmb, nblk_max):
    @pl.when(pl.program_id(0) == 0)
    def _():
        zero_sc[...] = jnp.zeros(zero_sc.shape, F32)

        def zero_block(row):
            return pltpu.make_async_copy(
                zero_sc, xs_ref.at[pl.ds(pl.multiple_of(row, tmb), tmb), :], zsem)

        for go in (lambda c: c.start(), lambda c: c.wait()):
            def seg_block(e, carry, go=go):
                @pl.when(lastblk_ref[e] >= 0)
                def _():
                    go(zero_block(lastblk_ref[e]))
                return carry

            def tail_block(b, carry, go=go):
                go(zero_block(b * tmb))
                return carry

            lax.fori_loop(0, N_EXPERTS, seg_block, 0)
            lax.fori_loop(nblk_ref[0], nblk_max, tail_block, 0)

    def start(r, carry):
        for k in range(2):
            dest = pstart_ref[route_ref[k, r]] + route_ref[2 + k, r]
            _row_copy(xn_ref, r, xs_ref, dest, sem).start()
        return carry

    lax.fori_loop(0, tm, start, 0, unroll=ROW_UNROLL)

    def wait(r, carry):
        for k in range(2):
            _row_copy(xn_ref, 0, xs_ref, 0, sem).wait()
        return carry

    lax.fori_loop(0, tm, wait, 0, unroll=ROW_UNROLL)


def _dispatch(pstart, lastblk, nblk, route, xn, *, tmb, nblk_max, tm):
    t, d_model = xn.shape
    kern = functools.partial(_dispatch_kernel, tm=tm, tmb=tmb, nblk_max=nblk_max)
    return pl.pallas_call(
        kern,
        out_shape=jax.ShapeDtypeStruct((nblk_max * tmb, d_model), F32),
        grid_spec=pltpu.PrefetchScalarGridSpec(
            num_scalar_prefetch=3,
            grid=(t // tm,),
            in_specs=[
                pl.BlockSpec((8, tm), lambda i, *_: (0, i), memory_space=pltpu.SMEM),
                pl.BlockSpec((tm, d_model), lambda i, *_: (i, 0)),
            ],
            out_specs=pl.BlockSpec(memory_space=pl.ANY),
            scratch_shapes=[pltpu.VMEM((tmb, d_model), F32), pltpu.SemaphoreType.DMA,
                            pltpu.SemaphoreType.DMA],
        ),
        compiler_params=_cparams(("arbitrary",)),
        name="dispatch",
    )(pstart, lastblk, nblk, route, xn)


def _combine_kernel(pstart_ref, route_ref, gates_ref, x1_ref, y_ref, o_ref, buf, sem, *, tm):
    def start(r, carry):
        for k in range(2):
            src = pstart_ref[route_ref[k, r]] + route_ref[2 + k, r]
            _row_copy(y_ref, src, buf.at[k], r, sem).start()
        return carry

    lax.fori_loop(0, tm, start, 0, unroll=ROW_UNROLL)

    def wait(r, carry):
        for k in range(2):
            _row_copy(y_ref, 0, buf.at[k], 0, sem).wait()
        return carry

    lax.fori_loop(0, tm, wait, 0, unroll=ROW_UNROLL)

    gt = jnp.transpose(gates_ref[...])
    o_ref[...] = x1_ref[...] + gt[:, 0:1] * buf[0] + gt[:, 1:2] * buf[1]


def _combine(pstart, route, gates, x1, y, *, tm):
    t, d_model = x1.shape
    kern = functools.partial(_combine_kernel, tm=tm)
    return pl.pallas_call(
        kern,
        out_shape=jax.ShapeDtypeStruct((t, d_model), F32),
        grid_spec=pltpu.PrefetchScalarGridSpec(
            num_scalar_prefetch=1,
            grid=(t // tm,),
            in_specs=[
                pl.BlockSpec((8, tm), lambda i, ps: (0, i), memory_space=pltpu.SMEM),
                pl.BlockSpec((8, tm), lambda i, ps: (0, i)),
                pl.BlockSpec((tm, d_model), lambda i, ps: (i, 0)),
                pl.BlockSpec(memory_space=pl.ANY),
            ],
            out_specs=pl.BlockSpec((tm, d_model), lambda i, ps: (i, 0)),
            scratch_shapes=[pltpu.VMEM((2, tm, d_model), F32), pltpu.SemaphoreType.DMA],
        ),
        compiler_params=_cparams(("arbitrary",)),
        name="combine",
    )(pstart, route, gates, x1, y)


def _experts_kernel(blk_e_ref, nblk_ref, xs_ref, wg_ref, wu_ref, wd_ref, y_ref,
                    wgu_sc, wd_sc, *, d_ff):
    b = pl.program_id(0)
    active = b < nblk_ref[0]

    @pl.when(active)
    def _():
        prev = blk_e_ref[jnp.maximum(b - 1, 0)]

        @pl.when(jnp.logical_or(b == 0, blk_e_ref[b] != prev))
        def _():
            wgu_sc[:, :d_ff] = wg_ref[0].astype(BF16)
            wgu_sc[:, d_ff:] = wu_ref[0].astype(BF16)
            wd_sc[...] = wd_ref[0].astype(BF16)

        gu = jnp.dot(xs_ref[...].astype(BF16), wgu_sc[...], preferred_element_type=F32)
        gate = gu[:, :d_ff]
        act = (gate * jax.nn.sigmoid(gate)) * gu[:, d_ff:]
        y_ref[...] = jnp.dot(act.astype(BF16), wd_sc[...], preferred_element_type=F32)

    @pl.when(jnp.logical_not(active))
    def _():
        y_ref[...] = jnp.zeros(y_ref.shape, F32)


def _experts(blk_e, nblk, xs, w_gate, w_up, w_down, *, tmb):
    rows, d_model = xs.shape
    d_ff = w_gate.shape[-1]
    kern = functools.partial(_experts_kernel, d_ff=d_ff)

    def x_map(b, be, nb):
        return (jnp.minimum(b, nb[0] - 1), 0)

    def w_map(b, be, nb):
        return (be[b], 0, 0)

    return pl.pallas_call(
        kern,
        out_shape=jax.ShapeDtypeStruct((rows, d_model), F32),
        grid_spec=pltpu.PrefetchScalarGridSpec(
            num_scalar_prefetch=2,
            grid=(rows // tmb,),
            in_specs=[
                pl.BlockSpec((tmb, d_model), x_map),
                pl.BlockSpec((1, d_model, d_ff), w_map),
                pl.BlockSpec((1, d_model, d_ff), w_map),
                pl.BlockSpec((1, d_ff, d_model), w_map),
            ],
            out_specs=pl.BlockSpec((tmb, d_model), lambda b, be, nb: (b, 0)),
            scratch_shapes=[pltpu.VMEM((d_model, 2 * d_ff), BF16), pltpu.VMEM((d_ff, d_model), BF16)],
        ),
        compiler_params=_cparams(("arbitrary",)),
        name="experts",
    )(blk_e, nblk, xs, w_gate, w_up, w_down)


class _Tiles:
    def __init__(self, t, s):
        self.proj = min(512, t)
        self.attn = min(256, s)
        self.rows = min(512, t)
        self.expert = min(256, t)


def _strict_upper(n):
    a = jnp.arange(n)
    return (a[:, None] < a[None, :]).astype(BF16)


def _block_table(counts, tmb, nblk_max):
    padded = ((counts + tmb - 1) // tmb) * tmb
    pend = jnp.cumsum(padded)
    pstart = (pend - padded).astype(I32)
    blk_first = jnp.arange(nblk_max, dtype=I32) * tmb
    blk_e = jnp.clip(jnp.searchsorted(pend, blk_first, side="right"), 0, N_EXPERTS - 1).astype(I32)
    nblk = (pend[-1] // tmb).astype(I32)
    last_e = blk_e[jnp.maximum(nblk - 1, 0)]
    blk_e = jnp.where(jnp.arange(nblk_max) < nblk, blk_e, last_e)
    lastblk = jnp.where(padded > 0, pend - tmb, -1).astype(I32)
    return pstart, lastblk, blk_e, nblk.reshape(1)


def kernel(x, rel_bias, ln1_g, w_in, qnorm_g, knorm_g, lambda_q1, lambda_k1, lambda_q2, lambda_k2,
           subln_g, w_branch_diff, w_branch_sb, w_out, ln2_g, w_group, w_router, w_gate, w_up, w_down):
    b, s, d_model = x.shape
    depth = w_in.shape[0]
    t = b * s
    tiles = _Tiles(t, s)
    n_rows = 2 * t
    nblk_max = n_rows // tiles.expert + N_EXPERTS

    bias = _bias_tiles(rel_bias, tiles.attn)
    head = jnp.arange(STACK_COLS) // HEAD_DIM
    seg = jnp.where(head[:, None] == head[None, :], 1.0 / HEAD_DIM, 0.0).astype(BF16)
    tri_attn = jnp.transpose(_strict_upper(tiles.attn))
    tri_tok = _strict_upper(tiles.proj)

    x2 = x.reshape(t, d_model)
    for l in range(depth):
        lam_init = 0.8 - 0.6 * math.exp(-0.3 * l)
        n_qk_heads = 2 * N_DIFF_HEADS
        qk_gain = jnp.concatenate([jnp.tile(qnorm_g[l] * (HEAD_DIM ** -0.5), n_qk_heads),
                                   jnp.tile(knorm_g[l], n_qk_heads)]).reshape(1, QK_COLS).astype(F32)
        lamv = jnp.stack([lambda_q1[l], lambda_k1[l], lambda_q2[l], lambda_k2[l]]).astype(F32)

        proj = _in_proj(x2, ln1_g[l].reshape(1, d_model), w_in[l].astype(BF16), qk_gain, seg,
                        tm=tiles.proj)
        proj3 = proj.reshape(b, s, proj.shape[-1])
        oa = _diff_attn(proj3, lamv, subln_g[l].reshape(1, DIFF_V_DIM), bias,
                        tq=tiles.attn, lam_init=lam_init)
        ob = _sb_attn(proj3, tri_attn, tq=tiles.attn)

        w_rt = jnp.zeros((ROUTER_ROWS, d_model), F32)
        w_rt = w_rt.at[:N_GROUPS].set(w_group[l].T).at[8:].set(w_router[l].T)
        wrh, wrl = _split_bf16(w_rt)
        x1, xn, route, gates, counts = _post_attn(
            x2, oa.reshape(t, VA_COLS), ob.reshape(t, SB_COLS), proj,
            w_branch_diff[l].astype(BF16), w_branch_sb[l].astype(BF16), w_out[l].astype(BF16),
            ln2_g[l].reshape(1, d_model), wrh, wrl, tri_tok, tm=tiles.proj)

        pstart, lastblk, blk_e, nblk = _block_table(counts[:, 0], tiles.expert, nblk_max)
        xs = _dispatch(pstart, lastblk, nblk, route, xn, tmb=tiles.expert, nblk_max=nblk_max,
                       tm=tiles.rows)
        y = _experts(blk_e, nblk, xs, w_gate[l], w_up[l], w_down[l], tmb=tiles.expert)
        x2 = _combine(pstart, route, gates, x1, y, tm=tiles.rows)
    return x2.reshape(b, s, d_model)
```

```python
import functools
import math

import jax
import jax.numpy as jnp
from jax import lax
from jax.experimental import pallas as pl
from jax.experimental.pallas import tpu as pltpu

F32 = jnp.float32
BF16 = jnp.bfloat16
I32 = jnp.int32

HEAD_DIM = 64
N_DIFF_HEADS = 4
DIFF_V_DIM = 2 * HEAD_DIM
N_SB_HEADS = 8
N_BUCKETS = 32
MAX_DISTANCE = 128
N_GROUPS = 4
EXPERTS_PER_GROUP = 8
N_EXPERTS = N_GROUPS * EXPERTS_PER_GROUP
EPS = 1e-6
NEG_INF = -1e30
LOG2_E = math.log2(math.e)

V7X_LANES = 128
VMEM_LIMIT = 56 * 1024 * 1024

F32_EXP_ZERO_BELOW = -104.0

HEADS_PER_STACK = 4
STACK = HEADS_PER_STACK * HEAD_DIM

QK_COLS = 4 * N_DIFF_HEADS * HEAD_DIM
VA_COLS = N_DIFF_HEADS * DIFF_V_DIM
SB_COLS = N_SB_HEADS * HEAD_DIM
NAT_K, NAT_KS, NAT_GATE = 0, 2 * STACK, 4 * STACK
T_Q, T_VA, T_QS, T_VS = 0, 2 * STACK, 4 * STACK, 6 * STACK
T_ROWS = 8 * STACK


def _cparams(semantics, vmem=VMEM_LIMIT):
    return pltpu.CompilerParams(dimension_semantics=semantics, vmem_limit_bytes=vmem)


def _nt_dot(a, b):
    return lax.dot_general(a, b, (((1,), (1,)), ((), ())), preferred_element_type=F32)


def _dot(a, b):
    return jnp.dot(a, b, preferred_element_type=F32)


def _split_bf16(x):
    hi = x.astype(BF16)
    lo = (x - hi.astype(F32)).astype(BF16)
    return hi, lo


def _in_proj_kernel(x_ref, g_ref, wn_ref, wt_ref, kg_ref, seg_ref, qg_ref, on_ref, ot_ref,
                    *, tm, d_model, blk):
    x = x_ref[...]
    ms = jnp.mean(x * x, axis=-1, keepdims=True)
    h = ((x * lax.rsqrt(ms + EPS)) * g_ref[...]).astype(BF16)

    seg = seg_ref[...]
    for col in range(NAT_K, NAT_KS, STACK):
        acc = _dot(h, wn_ref[:, col:col + STACK])
        hi, lo = _split_bf16(acc * acc)
        msq = _dot(hi, seg) + _dot(lo, seg)
        acc = (acc * lax.rsqrt(msq + EPS)) * kg_ref[:, col:col + STACK]
        on_ref[:, col:col + STACK] = acc.astype(BF16)
    width = 2 * STACK
    on_ref[:, NAT_KS:NAT_GATE] = _dot(h, wn_ref[:, NAT_KS:NAT_GATE]).astype(BF16)
    for col in range(NAT_GATE, NAT_GATE + 2 * d_model, width):
        on_ref[:, col:col + width] = jax.nn.sigmoid(_dot(h, wn_ref[:, col:col + width])).astype(BF16)

    for row in range(0, T_ROWS, width):
        acc = _nt_dot(wt_ref[row:row + width, :], h)
        if row == T_Q:
            a3 = acc.reshape(width // HEAD_DIM, HEAD_DIM, tm)
            msq = jnp.mean(a3 * a3, axis=1, keepdims=True)
            acc = (a3 * lax.rsqrt(msq + EPS)).reshape(width, tm)
            acc = acc * qg_ref[...]
        elif row == T_QS:
            acc = acc * (HEAD_DIM ** -0.5)
        for c in range(tm // blk):
            ot_ref[0, c, row:row + width, :] = acc[:, c * blk:(c + 1) * blk].astype(BF16)


def _in_proj(x2, ln_g, wn, wt, k_gain, seg, q_gain, *, batch, seq, tm, blk):
    t, d_model = x2.shape
    nat_cols = wn.shape[1]
    tiles_per_batch = seq // tm
    kern = functools.partial(_in_proj_kernel, tm=tm, d_model=d_model, blk=blk)
    const = lambda i: (0, 0)
    return pl.pallas_call(
        kern,
        out_shape=(jax.ShapeDtypeStruct((t, nat_cols), BF16),
                   jax.ShapeDtypeStruct((batch, seq // blk, T_ROWS, blk), BF16)),
        grid=(t // tm,),
        in_specs=[
            pl.BlockSpec((tm, d_model), lambda i: (i, 0)),
            pl.BlockSpec((1, d_model), const),
            pl.BlockSpec((d_model, nat_cols), const),
            pl.BlockSpec((T_ROWS, d_model), const),
            pl.BlockSpec((1, 2 * STACK), const),
            pl.BlockSpec((STACK, STACK), const),
            pl.BlockSpec((2 * STACK, tm), const),
        ],
        out_specs=(
            pl.BlockSpec((tm, nat_cols), lambda i: (i, 0)),
            pl.BlockSpec((1, tm // blk, T_ROWS, blk),
                         lambda i: (i // tiles_per_batch, i % tiles_per_batch, 0, 0)),
        ),
        compiler_params=_cparams(("arbitrary",)),
        name="in_proj",
    )(x2, ln_g, wn, wt, k_gain, seg, q_gain)


def _stack_heads_t(q_t, qs_sc, tq):
    qf = q_t.astype(F32)
    row_head = lax.broadcasted_iota(I32, (STACK, tq), 0) // HEAD_DIM
    for h in range(HEADS_PER_STACK):
        qs_sc[:, h * tq:(h + 1) * tq] = jnp.where(row_head == h, qf, 0.0).astype(BF16)


def _t5_bucket(n):
    max_exact = N_BUCKETS // 2
    nf = jnp.maximum(n, max_exact).astype(F32)
    large = max_exact + (jnp.log(nf / max_exact) / math.log(MAX_DISTANCE / max_exact)
                         * (N_BUCKETS - max_exact)).astype(I32)
    large = jnp.minimum(large, N_BUCKETS - 1)
    return jnp.where(n < max_exact, n, large)


def _bias_tiles_t(rel_bias, tq):
    assert tq + 1 >= MAX_DISTANCE, "far blocks must lie entirely in the last bucket"
    dist = jnp.arange(tq, dtype=I32)[None, :] - jnp.arange(tq, dtype=I32)[:, None]
    rb = rel_bias.astype(F32)

    def tile(d):
        onehot = (_t5_bucket(jnp.maximum(d, 0))[..., None] == jnp.arange(N_BUCKETS)).astype(F32)
        return jnp.einsum("kqb,bh->khq", onehot, rb, precision=lax.Precision.HIGHEST)

    far = tile(jnp.full((1, 1), 2 * tq, I32))
    near = tile(dist + tq) - far
    diag = jnp.where(dist[:, None, :] >= 0, tile(dist) - far, NEG_INF)
    return jnp.stack([near, diag]).reshape(2, tq, N_DIFF_HEADS * tq)


def _diff_attn_kernel(lam_ref, subg_ref, bias_ref, q_ref, k_ref, v_ref, o_ref,
                      qs_sc, m_sc, l_sc, acc_sc, *, tq, lam_init):
    i = pl.program_id(1)
    for mp in range(2):
        _stack_heads_t(q_ref[0, 0, mp * STACK:(mp + 1) * STACK, :], qs_sc.at[mp], tq)
    m_sc[...] = jnp.full(m_sc.shape, NEG_INF, F32)
    l_sc[...] = jnp.zeros(l_sc.shape, F32)
    acc_sc[...] = jnp.zeros(acc_sc.shape, F32)

    def step(j, bias_idx):
        start = pl.multiple_of(j * tq, tq)
        chains = [(mp, h) for mp in range(2) for h in range(N_DIFF_HEADS)]
        scores = []
        for mp, h in chains:
            kb = k_ref[0, pl.ds(start, tq), mp * STACK:(mp + 1) * STACK]
            scores.append(_dot(kb, qs_sc[mp, :, h * tq:(h + 1) * tq]))
        probs = []
        for (mp, h), s in zip(chains, scores):
            cols = slice(h * tq, (h + 1) * tq)
            if bias_idx is not None:
                s = s + bias_ref[bias_idx, :, cols]
            m_old = m_sc[mp, :, cols]
            m_new = jnp.maximum(m_old, jnp.max(s, axis=0, keepdims=True))
            alpha = jnp.exp2(m_old - m_new)
            p = jnp.exp2(s - m_new)
            l_sc[mp, :, cols] = alpha * l_sc[mp, :, cols] + jnp.sum(p, axis=0, keepdims=True)
            m_sc[mp, :, cols] = m_new
            probs.append((alpha, p.astype(BF16)))
        for (mp, h), (alpha, pb) in zip(chains, probs):
            rows = slice(h * DIFF_V_DIM, (h + 1) * DIFF_V_DIM)
            acc_sc[mp, rows, :] = alpha * acc_sc[mp, rows, :] + _dot(v_ref[0, j, rows, :], pb)

    def far_body(j, carry):
        step(j, None)
        return carry

    lax.fori_loop(0, i - 1, far_body, 0)

    @pl.when(i >= 1)
    def _():
        step(i - 1, 0)

    step(i, 1)

    lamv = lam_ref[...]
    lam = (jnp.exp(jnp.sum(lamv[0:1] * lamv[1:2], axis=1, keepdims=True))
           - jnp.exp(jnp.sum(lamv[2:3] * lamv[3:4], axis=1, keepdims=True)) + lam_init)
    inv_l1 = 1.0 / l_sc[0]
    inv_l2 = lam / l_sc[1]
    for h in range(N_DIFF_HEADS):
        rows = slice(h * DIFF_V_DIM, (h + 1) * DIFF_V_DIM)
        cols = slice(h * tq, (h + 1) * tq)
        o_t = acc_sc[0, rows, :] * inv_l1[:, cols] - acc_sc[1, rows, :] * inv_l2[:, cols]
        o = jnp.transpose(o_t)
        ms = jnp.mean(o * o, axis=-1, keepdims=True)
        o = ((o * lax.rsqrt(ms + EPS)) * subg_ref[...]) * (1.0 - lam_init)
        o_ref[0, :, rows] = o.astype(BF16)


def _diff_attn(nat3, proj_t, lamv, subln_g, bias, *, tq, lam_init):
    b, s, _ = nat3.shape
    nblk = s // tq
    m = N_DIFF_HEADS * tq
    kern = functools.partial(_diff_attn_kernel, tq=tq, lam_init=lam_init)
    two = 2 * STACK
    return pl.pallas_call(
        kern,
        out_shape=jax.ShapeDtypeStruct((b, s, VA_COLS), BF16),
        grid=(b, nblk),
        in_specs=[
            pl.BlockSpec((4, HEAD_DIM), lambda bi, i: (0, 0)),
            pl.BlockSpec((1, DIFF_V_DIM), lambda bi, i: (0, 0)),
            pl.BlockSpec((2, tq, m), lambda bi, i: (0, 0, 0)),
            pl.BlockSpec((1, 1, two, tq), lambda bi, i: (bi, i, T_Q // two, 0)),
            pl.BlockSpec((1, s, two), lambda bi, i: (bi, 0, NAT_K // two)),
            pl.BlockSpec((1, nblk, two, tq), lambda bi, i: (bi, 0, T_VA // two, 0)),
        ],
        out_specs=pl.BlockSpec((1, tq, VA_COLS), lambda bi, i: (bi, i, 0)),
        scratch_shapes=[
            pltpu.VMEM((2, STACK, m), BF16),
            pltpu.VMEM((2, 1, m), F32),
            pltpu.VMEM((2, 1, m), F32),
            pltpu.VMEM((2, VA_COLS, tq), F32),
        ],
        compiler_params=_cparams(("arbitrary", "arbitrary")),
        name="diff_attn",
    )(lamv, subln_g, bias, proj_t, nat3, proj_t)


def _sb_attn_kernel(tri_ref, q_ref, k_ref, v_ref, o_ref, qs_sc, acc_sc, c_sc, *, tq):
    i = pl.program_id(2)
    m = HEADS_PER_STACK * tq
    _stack_heads_t(q_ref[0, 0], qs_sc, tq)
    acc_sc[...] = jnp.zeros(acc_sc.shape, F32)
    c_sc[...] = jnp.zeros(c_sc.shape, F32)

    def block(j, on_diagonal):
        start = pl.multiple_of(j * tq, tq)
        kb = k_ref[0, pl.ds(start, tq), :]
        if on_diagonal:
            mask = (lax.broadcasted_iota(I32, (tq, tq), 0)
                    < lax.broadcasted_iota(I32, (tq, tq), 1))
        heads = range(HEADS_PER_STACK)
        tri = tri_ref[...]
        zs = [_dot(kb, qs_sc[:, h * tq:(h + 1) * tq]) for h in heads]
        parts = []
        for h in heads:
            z = zs[h]
            log_1m = -(jnp.maximum(z, 0.0) + jnp.log1p(jnp.exp(-jnp.abs(z))))
            log_sig = z + log_1m
            if on_diagonal:
                log_1m = jnp.where(mask, log_1m, 0.0)
            parts.append((log_sig, log_1m) + _split_bf16(log_1m))
        sufs = [_dot(tri, hi) + _dot(tri, lo) for _, _, hi, lo in parts]
        weights = []
        for h in heads:
            cols = slice(h * tq, (h + 1) * tq)
            log_sig, log_1m, _, _ = parts[h]
            a = jnp.exp(log_sig + (sufs[h] + c_sc[:, cols]))
            if on_diagonal:
                a = jnp.where(mask, a, 0.0)
            c_sc[:, cols] += jnp.sum(log_1m, axis=0, keepdims=True)
            weights.append(a.astype(BF16))
        for h in heads:
            rows = slice(h * HEAD_DIM, (h + 1) * HEAD_DIM)
            acc_sc[rows, :] += _dot(v_ref[0, j, rows, :], weights[h])

    block(i, True)

    def cond(j):
        return jnp.logical_and(j >= 0, jnp.max(c_sc[...]) > F32_EXP_ZERO_BELOW)

    def body(j):
        block(j, False)
        return j - 1

    lax.while_loop(cond, body, i - 1)

    o_ref[0] = jnp.transpose(acc_sc[...]).astype(BF16)


def _sb_attn(nat3, proj_t, tri, *, tq):
    b, s, _ = nat3.shape
    nblk = s // tq
    n_stacks = SB_COLS // STACK
    m = HEADS_PER_STACK * tq
    kern = functools.partial(_sb_attn_kernel, tq=tq)
    return pl.pallas_call(
        kern,
        out_shape=jax.ShapeDtypeStruct((b, s, SB_COLS), BF16),
        grid=(b, n_stacks, nblk),
        in_specs=[
            pl.BlockSpec((tq, tq), lambda bi, g, i: (0, 0)),
            pl.BlockSpec((1, 1, STACK, tq), lambda bi, g, i: (bi, i, T_QS // STACK + g, 0)),
            pl.BlockSpec((1, s, STACK), lambda bi, g, i: (bi, 0, NAT_KS // STACK + g)),
            pl.BlockSpec((1, nblk, STACK, tq), lambda bi, g, i: (bi, 0, T_VS // STACK + g, 0)),
        ],
        out_specs=pl.BlockSpec((1, tq, STACK), lambda bi, g, i: (bi, i, g)),
        scratch_shapes=[pltpu.VMEM((STACK, m), BF16), pltpu.VMEM((STACK, tq), F32),
                        pltpu.VMEM((1, m), F32)],
        compiler_params=_cparams(("arbitrary", "arbitrary", "arbitrary")),
        name="sb_attn",
    )(tri, proj_t, nat3, proj_t)


ROUTER_ROWS = 8 + N_EXPERTS


def _post_attn_kernel(x_ref, oa_ref, ob_ref, ga_ref, gb_ref, wbd_ref, wbs_ref, wo_ref, g_ref,
                      wrh_ref, wrl_ref, tri_ref,
                      x1_ref, xn_ref, route_ref, gates_ref, counts_ref, cnt_sc, *, tm, d_model):
    @pl.when(pl.program_id(0) == 0)
    def _():
        cnt_sc[...] = jnp.zeros(cnt_sc.shape, F32)

    ga = ga_ref[...].astype(F32)
    gb = gb_ref[...].astype(F32)
    mixed = ga * _dot(oa_ref[...], wbd_ref[...]) + gb * _dot(ob_ref[...], wbs_ref[...])
    x1 = x_ref[...] + _dot(mixed.astype(BF16), wo_ref[...])
    x1_ref[...] = x1
    ms = jnp.mean(x1 * x1, axis=-1, keepdims=True)
    xn = (x1 * lax.rsqrt(ms + EPS)) * g_ref[...]
    xn_ref[...] = xn

    xh, xl = _split_bf16(xn)
    wrh = wrh_ref[...]
    logits = _nt_dot(wrh, xh) + _nt_dot(wrh, xl) + _nt_dot(wrl_ref[...], xh)

    gl = [logits[r:r + 1] for r in range(N_GROUPS)]
    gmax = functools.reduce(jnp.maximum, gl)
    grp = jnp.full((1, tm), N_GROUPS - 1, I32)
    for r in range(N_GROUPS - 2, -1, -1):
        grp = jnp.where(gl[r] == gmax, r, grp)
    pg = 1.0 / functools.reduce(lambda a, b: a + b, [jnp.exp(v - gmax) for v in gl])

    el = logits[8:8 + EXPERTS_PER_GROUP]
    for r in range(1, N_GROUPS):
        lo = 8 + r * EXPERTS_PER_GROUP
        el = jnp.where(grp == r, logits[lo:lo + EXPERTS_PER_GROUP], el)
    ex = jnp.exp(el - jnp.max(el, axis=0, keepdims=True))
    prob = ex / jnp.sum(ex, axis=0, keepdims=True)
    sub = lax.broadcasted_iota(I32, (EXPERTS_PER_GROUP, tm), 0).astype(F32)
    none = float(EXPERTS_PER_GROUP)
    v1 = jnp.max(prob, axis=0, keepdims=True)
    i1 = jnp.min(jnp.where(prob == v1, sub, none), axis=0, keepdims=True)
    rest = jnp.where(sub == i1, -1.0, prob)
    v2 = jnp.max(rest, axis=0, keepdims=True)
    i2 = jnp.min(jnp.where(rest == v2, sub, none), axis=0, keepdims=True)
    denom = v1 + v2
    gate1 = pg * v1 / denom
    gate2 = pg * v2 / denom
    e1 = grp * EXPERTS_PER_GROUP + i1.astype(I32)
    e2 = grp * EXPERTS_PER_GROUP + i2.astype(I32)

    eio = lax.broadcasted_iota(I32, (N_EXPERTS, tm), 0)
    hit1 = eio == e1
    hit2 = eio == e2
    onehot = jnp.where(hit1, 1.0, 0.0) + jnp.where(hit2, 1.0, 0.0)
    prefix = _dot(onehot.astype(BF16), tri_ref[...]) + cnt_sc[...]
    r1 = jnp.sum(jnp.where(hit1, prefix, 0.0), axis=0, keepdims=True)
    r2 = jnp.sum(jnp.where(hit2, prefix, 0.0), axis=0, keepdims=True)
    cnt = cnt_sc[...] + jnp.sum(onehot, axis=1, keepdims=True)
    cnt_sc[...] = cnt
    counts_ref[...] = jnp.broadcast_to(cnt, counts_ref.shape).astype(I32)

    row = lax.broadcasted_iota(I32, (8, tm), 0)
    route_ref[...] = jnp.where(row == 0, e1, jnp.where(row == 1, e2, jnp.where(
        row == 2, r1.astype(I32), jnp.where(row == 3, r2.astype(I32), 0))))
    gates_ref[...] = jnp.where(row == 0, gate1, jnp.where(row == 1, gate2, 0.0))


def _post_attn(x2, oa, ob, nat, wbd, wbs, wo, ln_g, wrh, wrl, tri, *, tm):
    t, d_model = x2.shape
    kern = functools.partial(_post_attn_kernel, tm=tm, d_model=d_model)
    gate_blk = NAT_GATE // d_model
    assert gate_blk * d_model == NAT_GATE
    const = lambda i: (0, 0)
    return pl.pallas_call(
        kern,
        out_shape=(
            jax.ShapeDtypeStruct((t, d_model), F32),
            jax.ShapeDtypeStruct((t, d_model), F32),
            jax.ShapeDtypeStruct((8, t), I32),
            jax.ShapeDtypeStruct((8, t), F32),
            jax.ShapeDtypeStruct((N_EXPERTS, V7X_LANES), I32),
        ),
        grid=(t // tm,),
        in_specs=[
            pl.BlockSpec((tm, d_model), lambda i: (i, 0)),
            pl.BlockSpec((tm, VA_COLS), lambda i: (i, 0)),
            pl.BlockSpec((tm, SB_COLS), lambda i: (i, 0)),
            pl.BlockSpec((tm, d_model), lambda i: (i, gate_blk)),
            pl.BlockSpec((tm, d_model), lambda i: (i, gate_blk + 1)),
            pl.BlockSpec((VA_COLS, d_model), const),
            pl.BlockSpec((SB_COLS, d_model), const),
            pl.BlockSpec((d_model, d_model), const),
            pl.BlockSpec((1, d_model), const),
            pl.BlockSpec((ROUTER_ROWS, d_model), const),
            pl.BlockSpec((ROUTER_ROWS, d_model), const),
            pl.BlockSpec((tm, tm), const),
        ],
        out_specs=(
            pl.BlockSpec((tm, d_model), lambda i: (i, 0)),
            pl.BlockSpec((tm, d_model), lambda i: (i, 0)),
            pl.BlockSpec((8, tm), lambda i: (0, i)),
            pl.BlockSpec((8, tm), lambda i: (0, i)),
            pl.BlockSpec((N_EXPERTS, V7X_LANES), const),
        ),
        scratch_shapes=[pltpu.VMEM((N_EXPERTS, 1), F32)],
        compiler_params=_cparams(("arbitrary",)),
        name="post_attn",
    )(x2, oa, ob, nat, nat, wbd, wbs, wo, ln_g, wrh, wrl, tri)


ROW_UNROLL = 8


def _row_copy(src, src_row, dst, dst_row, sem):
    return pltpu.make_async_copy(src.at[pl.ds(src_row, 1), :], dst.at[pl.ds(dst_row, 1), :], sem)


def _dispatch_kernel(pstart_ref, lastblk_ref, nblk_ref, route_ref, xn_ref, xs_ref, zero_sc, sem, zsem,
                     *, tm, tmb, nblk_max):
    @pl.when(pl.program_id(0) == 0)
    def _():
        zero_sc[...] = jnp.zeros(zero_sc.shape, F32)

        def zero_block(row):
            return pltpu.make_async_copy(
                zero_sc, xs_ref.at[pl.ds(pl.multiple_of(row, tmb), tmb), :], zsem)

        for go in (lambda c: c.start(), lambda c: c.wait()):
            def seg_block(e, carry, go=go):
                @pl.when(lastblk_ref[e] >= 0)
                def _():
                    go(zero_block(lastblk_ref[e]))
                return carry

            def tail_block(b, carry, go=go):
                go(zero_block(b * tmb))
                return carry

            lax.fori_loop(0, N_EXPERTS, seg_block, 0)
            lax.fori_loop(nblk_ref[0], nblk_max, tail_block, 0)

    def start(r, carry):
        for k in range(2):
            dest = pstart_ref[route_ref[k, r]] + route_ref[2 + k, r]
            _row_copy(xn_ref, r, xs_ref, dest, sem).start()
        return carry

    lax.fori_loop(0, tm, start, 0, unroll=ROW_UNROLL)

    def wait(r, carry):
        for k in range(2):
            _row_copy(xn_ref, 0, xs_ref, 0, sem).wait()
        return carry

    lax.fori_loop(0, tm, wait, 0, unroll=ROW_UNROLL)


def _dispatch(pstart, lastblk, nblk, route, xn, *, tmb, nblk_max, tm):
    t, d_model = xn.shape
    kern = functools.partial(_dispatch_kernel, tm=tm, tmb=tmb, nblk_max=nblk_max)
    return pl.pallas_call(
        kern,
        out_shape=jax.ShapeDtypeStruct((nblk_max * tmb, d_model), F32),
        grid_spec=pltpu.PrefetchScalarGridSpec(
            num_scalar_prefetch=3,
            grid=(t // tm,),
            in_specs=[
                pl.BlockSpec((8, tm), lambda i, *_: (0, i), memory_space=pltpu.SMEM),
                pl.BlockSpec((tm, d_model), lambda i, *_: (i, 0)),
            ],
            out_specs=pl.BlockSpec(memory_space=pl.ANY),
            scratch_shapes=[pltpu.VMEM((tmb, d_model), F32), pltpu.SemaphoreType.DMA,
                            pltpu.SemaphoreType.DMA],
        ),
        compiler_params=_cparams(("arbitrary",)),
        name="dispatch",
    )(pstart, lastblk, nblk, route, xn)


def _combine_kernel(pstart_ref, route_ref, gates_ref, x1_ref, y_ref, o_ref, buf, sem, *, tm):
    def start(r, carry):
        for k in range(2):
            src = pstart_ref[route_ref[k, r]] + route_ref[2 + k, r]
            _row_copy(y_ref, src, buf.at[k], r, sem).start()
        return carry

    lax.fori_loop(0, tm, start, 0, unroll=ROW_UNROLL)

    def wait(r, carry):
        for k in range(2):
            _row_copy(y_ref, 0, buf.at[k], 0, sem).wait()
        return carry

    lax.fori_loop(0, tm, wait, 0, unroll=ROW_UNROLL)

    gt = jnp.transpose(gates_ref[...])
    o_ref[...] = x1_ref[...] + gt[:, 0:1] * buf[0] + gt[:, 1:2] * buf[1]


def _combine(pstart, route, gates, x1, y, *, tm):
    t, d_model = x1.shape
    kern = functools.partial(_combine_kernel, tm=tm)
    return pl.pallas_call(
        kern,
        out_shape=jax.ShapeDtypeStruct((t, d_model), F32),
        grid_spec=pltpu.PrefetchScalarGridSpec(
            num_scalar_prefetch=1,
            grid=(t // tm,),
            in_specs=[
                pl.BlockSpec((8, tm), lambda i, ps: (0, i), memory_space=pltpu.SMEM),
                pl.BlockSpec((8, tm), lambda i, ps: (0, i)),
                pl.BlockSpec((tm, d_model), lambda i, ps: (i, 0)),
                pl.BlockSpec(memory_space=pl.ANY),
            ],
            out_specs=pl.BlockSpec((tm, d_model), lambda i, ps: (i, 0)),
            scratch_shapes=[pltpu.VMEM((2, tm, d_model), F32), pltpu.SemaphoreType.DMA],
        ),
        compiler_params=_cparams(("arbitrary",)),
        name="combine",
    )(pstart, route, gates, x1, y)


def _experts_kernel(blk_e_ref, nblk_ref, xs_ref, wg_ref, wu_ref, wd_ref, y_ref,
                    wgu_sc, wd_sc, *, d_ff):
    b = pl.program_id(0)
    active = b < nblk_ref[0]

    @pl.when(active)
    def _():
        prev = blk_e_ref[jnp.maximum(b - 1, 0)]

        @pl.when(jnp.logical_or(b == 0, blk_e_ref[b] != prev))
        def _():
            wgu_sc[:, :d_ff] = wg_ref[0, 0].astype(BF16)
            wgu_sc[:, d_ff:] = wu_ref[0, 0].astype(BF16)
            wd_sc[...] = wd_ref[0, 0].astype(BF16)

        gu = _dot(xs_ref[...].astype(BF16), wgu_sc[...])
        gate = gu[:, :d_ff]
        act = (gate * jax.nn.sigmoid(gate)) * gu[:, d_ff:]
        y_ref[...] = _dot(act.astype(BF16), wd_sc[...])

    @pl.when(jnp.logical_not(active))
    def _():
        y_ref[...] = jnp.zeros(y_ref.shape, F32)


def _experts(blk_e, nblk, xs, w_gate, w_up, w_down, *, layer, tmb):
    rows, d_model = xs.shape
    d_ff = w_gate.shape[-1]
    kern = functools.partial(_experts_kernel, d_ff=d_ff)

    def x_map(b, be, nb):
        return (jnp.minimum(b, nb[0] - 1), 0)

    def w_map(b, be, nb):
        return (layer, be[b], 0, 0)

    return pl.pallas_call(
        kern,
        out_shape=jax.ShapeDtypeStruct((rows, d_model), F32),
        grid_spec=pltpu.PrefetchScalarGridSpec(
            num_scalar_prefetch=2,
            grid=(rows // tmb,),
            in_specs=[
                pl.BlockSpec((tmb, d_model), x_map),
                pl.BlockSpec((1, 1, d_model, d_ff), w_map),
                pl.BlockSpec((1, 1, d_model, d_ff), w_map),
                pl.BlockSpec((1, 1, d_ff, d_model), w_map),
            ],
            out_specs=pl.BlockSpec((tmb, d_model), lambda b, be, nb: (b, 0)),
            scratch_shapes=[pltpu.VMEM((d_model, 2 * d_ff), BF16), pltpu.VMEM((d_ff, d_model), BF16)],
        ),
        compiler_params=_cparams(("arbitrary",)),
        name="experts",
    )(blk_e, nblk, xs, w_gate, w_up, w_down)


class _Tiles:
    def __init__(self, t, s):
        self.attn = min(256, s)
        self.proj = min(512, s)
        self.rows = min(512, t)
        self.expert = min(256, t)


def _strict_upper(n):
    a = jnp.arange(n)
    return (a[:, None] < a[None, :]).astype(BF16)


def _block_table(counts, tmb, nblk_max):
    padded = ((counts + tmb - 1) // tmb) * tmb
    pend = jnp.cumsum(padded)
    pstart = (pend - padded).astype(I32)
    blk_first = jnp.arange(nblk_max, dtype=I32) * tmb
    blk_e = jnp.minimum(jnp.sum(pend[None, :] <= blk_first[:, None], axis=1), N_EXPERTS - 1).astype(I32)
    nblk = (pend[-1] // tmb).astype(I32)
    last_e = jnp.sum(jnp.where(jnp.arange(nblk_max) == nblk - 1, blk_e, 0))
    blk_e = jnp.where(jnp.arange(nblk_max) < nblk, blk_e, last_e)
    lastblk = jnp.where(padded > 0, pend - tmb, -1).astype(I32)
    return pstart, lastblk, blk_e, nblk.reshape(1)


def kernel(x, rel_bias, ln1_g, w_in, qnorm_g, knorm_g, lambda_q1, lambda_k1, lambda_q2, lambda_k2,
           subln_g, w_branch_diff, w_branch_sb, w_out, ln2_g, w_group, w_router, w_gate, w_up, w_down):
    b, s, d_model = x.shape
    depth = w_in.shape[0]
    t = b * s
    tiles = _Tiles(t, s)
    nblk_max = 2 * t // tiles.expert + N_EXPERTS

    bias = _bias_tiles_t(rel_bias, tiles.attn) * LOG2_E
    head = jnp.arange(STACK) // HEAD_DIM
    seg = jnp.where(head[:, None] == head[None, :], 1.0 / HEAD_DIM, 0.0).astype(BF16)
    tri_attn = _strict_upper(tiles.attn)
    tri_tok = _strict_upper(tiles.proj)

    c_k, c_va = 2 * STACK, QK_COLS
    c_qs = c_va + VA_COLS
    c_ks, c_vs, c_g = c_qs + SB_COLS, c_qs + 2 * SB_COLS, c_qs + 3 * SB_COLS

    x2 = x.reshape(t, d_model)
    for l in range(depth):
        lam_init = 0.8 - 0.6 * math.exp(-0.3 * l)
        w = w_in[l].astype(BF16)
        wn = jnp.concatenate([w[:, c_k:c_va], w[:, c_ks:c_vs], w[:, c_g:]], axis=1)
        wt = jnp.transpose(jnp.concatenate(
            [w[:, :c_k], w[:, c_va:c_qs], w[:, c_qs:c_ks], w[:, c_vs:c_g]], axis=1))
        n_heads = 2 * N_DIFF_HEADS
        k_gain = jnp.tile(knorm_g[l].astype(F32), n_heads).reshape(1, 2 * STACK)
        q_gain = jnp.tile(qnorm_g[l].astype(F32) * (HEAD_DIM ** -0.5 * LOG2_E), n_heads)
        q_gain = jnp.broadcast_to(q_gain[:, None], (2 * STACK, tiles.proj))
        lamv = jnp.stack([lambda_q1[l], lambda_k1[l], lambda_q2[l], lambda_k2[l]]).astype(F32)

        nat, proj_t = _in_proj(x2, ln1_g[l].reshape(1, d_model), wn, wt, k_gain, seg, q_gain,
                               batch=b, seq=s, tm=tiles.proj, blk=tiles.attn)
        nat3 = nat.reshape(b, s, nat.shape[-1])
        oa = _diff_attn(nat3, proj_t, lamv, subln_g[l].reshape(1, DIFF_V_DIM), bias,
                        tq=tiles.attn, lam_init=lam_init)
        ob = _sb_attn(nat3, proj_t, tri_attn, tq=tiles.attn)

        w_rt = jnp.concatenate([w_group[l].T, jnp.zeros((8 - N_GROUPS, d_model), F32), w_router[l].T])
        wrh, wrl = _split_bf16(w_rt.astype(F32))
        x1, xn, route, gates, counts = _post_attn(
            x2, oa.reshape(t, VA_COLS), ob.reshape(t, SB_COLS), nat,
            w_branch_diff[l].astype(BF16), w_branch_sb[l].astype(BF16), w_out[l].astype(BF16),
            ln2_g[l].reshape(1, d_model), wrh, wrl, tri_tok, tm=tiles.proj)

        pstart, lastblk, blk_e, nblk = _block_table(counts[:, 0], tiles.expert, nblk_max)
        xs = _dispatch(pstart, lastblk, nblk, route, xn, tmb=tiles.expert, nblk_max=nblk_max,
                       tm=tiles.rows)
        y = _experts(blk_e, nblk, xs, w_gate, w_up, w_down, layer=l, tmb=tiles.expert)
        x2 = _combine(pstart, route, gates, x1, y, tm=tiles.rows)
    return x2.reshape(b, s, d_model)
```

```python
import functools
import math

import jax
import jax.numpy as jnp
from jax import lax
from jax.experimental import pallas as pl
from jax.experimental.pallas import tpu as pltpu

F32 = jnp.float32
BF16 = jnp.bfloat16
I32 = jnp.int32

HEAD_DIM = 64
N_DIFF_HEADS = 4
DIFF_V_DIM = 2 * HEAD_DIM
N_SB_HEADS = 8
N_BUCKETS = 32
MAX_DISTANCE = 128
N_GROUPS = 4
EXPERTS_PER_GROUP = 8
N_EXPERTS = N_GROUPS * EXPERTS_PER_GROUP
EPS = 1e-6
NEG_INF = -1e30
LOG2_E = math.log2(math.e)

V7X_LANES = 128
VMEM_LIMIT = 56 * 1024 * 1024

F32_EXP_ZERO_BELOW = -104.0

HEADS_PER_STACK = 4
STACK = HEADS_PER_STACK * HEAD_DIM

QK_COLS = 4 * N_DIFF_HEADS * HEAD_DIM
VA_COLS = N_DIFF_HEADS * DIFF_V_DIM
SB_COLS = N_SB_HEADS * HEAD_DIM
NAT_K, NAT_KS, NAT_GATE = 0, 2 * STACK, 4 * STACK
T_Q, T_VA, T_QS, T_VS = 0, 2 * STACK, 4 * STACK, 6 * STACK
T_ROWS = 8 * STACK


def _cparams(semantics, vmem=VMEM_LIMIT):
    return pltpu.CompilerParams(dimension_semantics=semantics, vmem_limit_bytes=vmem)


def _nt_dot(a, b):
    return lax.dot_general(a, b, (((1,), (1,)), ((), ())), preferred_element_type=F32)


def _dot(a, b):
    return jnp.dot(a, b, preferred_element_type=F32)


def _split_bf16(x):
    hi = x.astype(BF16)
    lo = (x - hi.astype(F32)).astype(BF16)
    return hi, lo


def _in_proj_kernel(x_ref, g_ref, wn_ref, wt_ref, kg_ref, seg_ref, qg_ref, on_ref, ot_ref,
                    *, tm, d_model, blk):
    x = x_ref[...]
    ms = jnp.mean(x * x, axis=-1, keepdims=True)
    h = ((x * lax.rsqrt(ms + EPS)) * g_ref[...]).astype(BF16)

    seg = seg_ref[...]
    for col in range(NAT_K, NAT_KS, STACK):
        acc = _dot(h, wn_ref[:, col:col + STACK])
        hi, lo = _split_bf16(acc * acc)
        msq = _dot(hi, seg) + _dot(lo, seg)
        acc = (acc * lax.rsqrt(msq + EPS)) * kg_ref[:, col:col + STACK]
        on_ref[:, col:col + STACK] = acc.astype(BF16)
    width = 2 * STACK
    on_ref[:, NAT_KS:NAT_GATE] = _dot(h, wn_ref[:, NAT_KS:NAT_GATE]).astype(BF16)
    for col in range(NAT_GATE, NAT_GATE + 2 * d_model, width):
        on_ref[:, col:col + width] = jax.nn.sigmoid(_dot(h, wn_ref[:, col:col + width])).astype(BF16)

    for row in range(0, T_ROWS, width):
        acc = _nt_dot(wt_ref[row:row + width, :], h)
        if row == T_Q:
            a3 = acc.reshape(width // HEAD_DIM, HEAD_DIM, tm)
            msq = jnp.mean(a3 * a3, axis=1, keepdims=True)
            acc = (a3 * lax.rsqrt(msq + EPS)).reshape(width, tm)
            acc = acc * qg_ref[...]
        elif row == T_QS:
            acc = acc * (HEAD_DIM ** -0.5)
        for c in range(tm // blk):
            ot_ref[0, c, row:row + width, :] = acc[:, c * blk:(c + 1) * blk].astype(BF16)


def _in_proj(x2, ln_g, wn, wt, k_gain, seg, q_gain, *, batch, seq, tm, blk):
    t, d_model = x2.shape
    nat_cols = wn.shape[1]
    tiles_per_batch = seq // tm
    kern = functools.partial(_in_proj_kernel, tm=tm, d_model=d_model, blk=blk)
    const = lambda i: (0, 0)
    return pl.pallas_call(
        kern,
        out_shape=(jax.ShapeDtypeStruct((t, nat_cols), BF16),
                   jax.ShapeDtypeStruct((batch, seq // blk, T_ROWS, blk), BF16)),
        grid=(t // tm,),
        in_specs=[
            pl.BlockSpec((tm, d_model), lambda i: (i, 0)),
            pl.BlockSpec((1, d_model), const),
            pl.BlockSpec((d_model, nat_cols), const),
            pl.BlockSpec((T_ROWS, d_model), const),
            pl.BlockSpec((1, 2 * STACK), const),
            pl.BlockSpec((STACK, STACK), const),
            pl.BlockSpec((2 * STACK, tm), const),
        ],
        out_specs=(
            pl.BlockSpec((tm, nat_cols), lambda i: (i, 0)),
            pl.BlockSpec((1, tm // blk, T_ROWS, blk),
                         lambda i: (i // tiles_per_batch, i % tiles_per_batch, 0, 0)),
        ),
        compiler_params=_cparams(("arbitrary",)),
        name="in_proj",
    )(x2, ln_g, wn, wt, k_gain, seg, q_gain)


def _stack_heads_t(q_t, qs_sc, tq):
    qf = q_t.astype(F32)
    row_head = lax.broadcasted_iota(I32, (STACK, tq), 0) // HEAD_DIM
    for h in range(HEADS_PER_STACK):
        qs_sc[:, h * tq:(h + 1) * tq] = jnp.where(row_head == h, qf, 0.0).astype(BF16)


def _t5_bucket(n):
    max_exact = N_BUCKETS // 2
    nf = jnp.maximum(n, max_exact).astype(F32)
    large = max_exact + (jnp.log(nf / max_exact) / math.log(MAX_DISTANCE / max_exact)
                         * (N_BUCKETS - max_exact)).astype(I32)
    large = jnp.minimum(large, N_BUCKETS - 1)
    return jnp.where(n < max_exact, n, large)


def _bias_tiles_t(rel_bias, tq):
    assert tq + 1 >= MAX_DISTANCE, "far blocks must lie entirely in the last bucket"
    dist = jnp.arange(tq, dtype=I32)[None, :] - jnp.arange(tq, dtype=I32)[:, None]
    rb = rel_bias.astype(F32)

    def tile(d):
        onehot = (_t5_bucket(jnp.maximum(d, 0))[..., None] == jnp.arange(N_BUCKETS)).astype(F32)
        return jnp.einsum("kqb,bh->khq", onehot, rb, precision=lax.Precision.HIGHEST)

    far = tile(jnp.full((1, 1), 2 * tq, I32))
    near = tile(dist + tq) - far
    diag = jnp.where(dist[:, None, :] >= 0, tile(dist) - far, NEG_INF)
    return jnp.stack([near, diag]).reshape(2, tq, N_DIFF_HEADS * tq)


ONES_ROWS = 16
FAR_BLOCKS = 2


def _diff_attn_kernel(lam_ref, subg_ref, bias_ref, q_ref, k_ref, v_ref, o_ref,
                      qs_sc, m_sc, acc_sc, *, tq, lam_init):
    i = pl.program_id(1)
    for mp in range(2):
        _stack_heads_t(q_ref[0, 0, mp * STACK:(mp + 1) * STACK, :], qs_sc.at[mp], tq)
    m_sc[...] = jnp.full(m_sc.shape, NEG_INF, F32)
    acc_sc[...] = jnp.zeros(acc_sc.shape, F32)
    ones = jnp.ones((ONES_ROWS, tq), BF16)

    def step(j, nb, bias_idx):
        start = pl.multiple_of(j * tq, tq)
        chains = [(mp, h) for mp in range(2) for h in range(N_DIFF_HEADS)]
        scores = []
        for mp, h in chains:
            kb = k_ref[0, pl.ds(start, nb * tq), mp * STACK:(mp + 1) * STACK]
            scores.append(_dot(kb, qs_sc[mp, :, h * tq:(h + 1) * tq]))
        probs = []
        for (mp, h), s in zip(chains, scores):
            cols = slice(h * tq, (h + 1) * tq)
            if bias_idx is not None:
                s = s + bias_ref[bias_idx, :, cols]
            m_old = m_sc[mp, :, cols]
            m_new = jnp.maximum(m_old, jnp.max(s, axis=0, keepdims=True))
            m_sc[mp, :, cols] = m_new
            probs.append((jnp.exp2(m_old - m_new), jnp.exp2(s - m_new).astype(BF16)))
        for (mp, h), (alpha, pb) in zip(chains, probs):
            rows = slice(h * DIFF_V_DIM, (h + 1) * DIFF_V_DIM)
            pv = None
            for blk in range(nb):
                v_ext = jnp.concatenate([v_ref[0, j + blk, rows, :], ones], axis=0)
                part = _dot(v_ext, pb[blk * tq:(blk + 1) * tq])
                pv = part if pv is None else pv + part
            acc_sc[mp, h] = alpha * acc_sc[mp, h] + pv

    n_far = jnp.maximum(i - 1, 0)

    def far_body(jj, carry):
        step(FAR_BLOCKS * jj, FAR_BLOCKS, None)
        return carry

    lax.fori_loop(0, n_far // FAR_BLOCKS, far_body, 0)
    for rem in range(1, FAR_BLOCKS):
        @pl.when(n_far % FAR_BLOCKS == rem)
        def _(rem=rem):
            step(n_far - rem, rem, None)

    @pl.when(i >= 1)
    def _():
        step(i - 1, 1, 0)

    step(i, 1, 1)

    lamv = lam_ref[...]
    lam = (jnp.exp(jnp.sum(lamv[0:1] * lamv[1:2], axis=1, keepdims=True))
           - jnp.exp(jnp.sum(lamv[2:3] * lamv[3:4], axis=1, keepdims=True)) + lam_init)
    for h in range(N_DIFF_HEADS):
        rows = slice(h * DIFF_V_DIM, (h + 1) * DIFF_V_DIM)
        a1 = acc_sc[0, h]
        a2 = acc_sc[1, h]
        inv_l1 = 1.0 / a1[DIFF_V_DIM:DIFF_V_DIM + 1]
        inv_l2 = lam / a2[DIFF_V_DIM:DIFF_V_DIM + 1]
        o_t = a1[:DIFF_V_DIM] * inv_l1 - a2[:DIFF_V_DIM] * inv_l2
        o = jnp.transpose(o_t)
        ms = jnp.mean(o * o, axis=-1, keepdims=True)
        o = ((o * lax.rsqrt(ms + EPS)) * subg_ref[...]) * (1.0 - lam_init)
        o_ref[0, :, rows] = o.astype(BF16)


def _diff_attn(nat3, proj_t, lamv, subln_g, bias, *, tq, lam_init):
    b, s, _ = nat3.shape
    nblk = s // tq
    m = N_DIFF_HEADS * tq
    kern = functools.partial(_diff_attn_kernel, tq=tq, lam_init=lam_init)
    two = 2 * STACK
    return pl.pallas_call(
        kern,
        out_shape=jax.ShapeDtypeStruct((b, s, VA_COLS), BF16),
        grid=(b, nblk),
        in_specs=[
            pl.BlockSpec((4, HEAD_DIM), lambda bi, i: (0, 0)),
            pl.BlockSpec((1, DIFF_V_DIM), lambda bi, i: (0, 0)),
            pl.BlockSpec((2, tq, m), lambda bi, i: (0, 0, 0)),
            pl.BlockSpec((1, 1, two, tq), lambda bi, i: (bi, i, T_Q // two, 0)),
            pl.BlockSpec((1, s, two), lambda bi, i: (bi, 0, NAT_K // two)),
            pl.BlockSpec((1, nblk, two, tq), lambda bi, i: (bi, 0, T_VA // two, 0)),
        ],
        out_specs=pl.BlockSpec((1, tq, VA_COLS), lambda bi, i: (bi, i, 0)),
        scratch_shapes=[
            pltpu.VMEM((2, STACK, m), BF16),
            pltpu.VMEM((2, 1, m), F32),
            pltpu.VMEM((2, N_DIFF_HEADS, DIFF_V_DIM + ONES_ROWS, tq), F32),
        ],
        compiler_params=_cparams(("arbitrary", "arbitrary")),
        name="diff_attn",
    )(lamv, subln_g, bias, proj_t, nat3, proj_t)


def _sb_attn_kernel(tri_ref, q_ref, k_ref, v_ref, o_ref, qs_sc, acc_sc, c_sc, *, tq):
    i = pl.program_id(2)
    m = HEADS_PER_STACK * tq
    _stack_heads_t(q_ref[0, 0], qs_sc, tq)
    acc_sc[...] = jnp.zeros(acc_sc.shape, F32)
    c_sc[...] = jnp.zeros(c_sc.shape, F32)

    def block(j, on_diagonal):
        start = pl.multiple_of(j * tq, tq)
        kb = k_ref[0, pl.ds(start, tq), :]
        if on_diagonal:
            mask = (lax.broadcasted_iota(I32, (tq, tq), 0)
                    < lax.broadcasted_iota(I32, (tq, tq), 1))
        heads = range(HEADS_PER_STACK)
        tri = tri_ref[...]
        zs = [_dot(kb, qs_sc[:, h * tq:(h + 1) * tq]) for h in heads]
        parts = []
        for h in heads:
            z = zs[h]
            nz = -z
            log_1m = jnp.minimum(nz, 0.0) - jnp.log(1.0 + jnp.exp(jnp.minimum(z, nz)))
            log_sig = z + log_1m
            if on_diagonal:
                log_1m = jnp.where(mask, log_1m, 0.0)
            parts.append((log_sig, log_1m) + _split_bf16(log_1m))
        sufs = [_dot(tri, hi) + _dot(tri, lo) for _, _, hi, lo in parts]
        weights = []
        for h in heads:
            cols = slice(h * tq, (h + 1) * tq)
            log_sig, log_1m, _, _ = parts[h]
            a = jnp.exp(log_sig + (sufs[h] + c_sc[:, cols]))
            if on_diagonal:
                a = jnp.where(mask, a, 0.0)
            c_sc[:, cols] += jnp.sum(log_1m, axis=0, keepdims=True)
            weights.append(a.astype(BF16))
        for h in heads:
            rows = slice(h * HEAD_DIM, (h + 1) * HEAD_DIM)
            acc_sc[rows, :] += _dot(v_ref[0, j, rows, :], weights[h])

    block(i, True)

    def cond(j):
        return jnp.logical_and(j >= 0, jnp.max(c_sc[...]) > F32_EXP_ZERO_BELOW)

    def body(j):
        block(j, False)
        return j - 1

    lax.while_loop(cond, body, i - 1)

    o_ref[0] = jnp.transpose(acc_sc[...]).astype(BF16)


def _sb_attn(nat3, proj_t, tri, *, tq):
    b, s, _ = nat3.shape
    nblk = s // tq
    n_stacks = SB_COLS // STACK
    m = HEADS_PER_STACK * tq
    kern = functools.partial(_sb_attn_kernel, tq=tq)
    return pl.pallas_call(
        kern,
        out_shape=jax.ShapeDtypeStruct((b, s, SB_COLS), BF16),
        grid=(b, n_stacks, nblk),
        in_specs=[
            pl.BlockSpec((tq, tq), lambda bi, g, i: (0, 0)),
            pl.BlockSpec((1, 1, STACK, tq), lambda bi, g, i: (bi, i, T_QS // STACK + g, 0)),
            pl.BlockSpec((1, s, STACK), lambda bi, g, i: (bi, 0, NAT_KS // STACK + g)),
            pl.BlockSpec((1, nblk, STACK, tq), lambda bi, g, i: (bi, 0, T_VS // STACK + g, 0)),
        ],
        out_specs=pl.BlockSpec((1, tq, STACK), lambda bi, g, i: (bi, i, g)),
        scratch_shapes=[pltpu.VMEM((STACK, m), BF16), pltpu.VMEM((STACK, tq), F32),
                        pltpu.VMEM((1, m), F32)],
        compiler_params=_cparams(("arbitrary", "arbitrary", "arbitrary")),
        name="sb_attn",
    )(tri, proj_t, nat3, proj_t)


ROUTER_ROWS = 8 + N_EXPERTS


def _post_attn_kernel(x_ref, oa_ref, ob_ref, ga_ref, gb_ref, wbd_ref, wbs_ref, wo_ref, g_ref,
                      wrh_ref, wrl_ref, tri_ref,
                      x1_ref, xn_ref, route_ref, gates_ref, counts_ref, cnt_sc, *, tm, d_model):
    @pl.when(pl.program_id(0) == 0)
    def _():
        cnt_sc[...] = jnp.zeros(cnt_sc.shape, F32)

    ga = ga_ref[...].astype(F32)
    gb = gb_ref[...].astype(F32)
    mixed = ga * _dot(oa_ref[...], wbd_ref[...]) + gb * _dot(ob_ref[...], wbs_ref[...])
    x1 = x_ref[...] + _dot(mixed.astype(BF16), wo_ref[...])
    x1_ref[...] = x1
    ms = jnp.mean(x1 * x1, axis=-1, keepdims=True)
    xn = (x1 * lax.rsqrt(ms + EPS)) * g_ref[...]
    xn_ref[...] = xn

    xh, xl = _split_bf16(xn)
    wrh = wrh_ref[...]
    logits = _nt_dot(wrh, xh) + _nt_dot(wrh, xl) + _nt_dot(wrl_ref[...], xh)

    gl = [logits[r:r + 1] for r in range(N_GROUPS)]
    gmax = functools.reduce(jnp.maximum, gl)
    grp = jnp.full((1, tm), N_GROUPS - 1, I32)
    for r in range(N_GROUPS - 2, -1, -1):
        grp = jnp.where(gl[r] == gmax, r, grp)
    pg = 1.0 / functools.reduce(lambda a, b: a + b, [jnp.exp(v - gmax) for v in gl])

    el = logits[8:8 + EXPERTS_PER_GROUP]
    for r in range(1, N_GROUPS):
        lo = 8 + r * EXPERTS_PER_GROUP
        el = jnp.where(grp == r, logits[lo:lo + EXPERTS_PER_GROUP], el)
    ex = jnp.exp(el - jnp.max(el, axis=0, keepdims=True))
    prob = ex / jnp.sum(ex, axis=0, keepdims=True)
    sub = lax.broadcasted_iota(I32, (EXPERTS_PER_GROUP, tm), 0).astype(F32)
    none = float(EXPERTS_PER_GROUP)
    v1 = jnp.max(prob, axis=0, keepdims=True)
    i1 = jnp.min(jnp.where(prob == v1, sub, none), axis=0, keepdims=True)
    rest = jnp.where(sub == i1, -1.0, prob)
    v2 = jnp.max(rest, axis=0, keepdims=True)
    i2 = jnp.min(jnp.where(rest == v2, sub, none), axis=0, keepdims=True)
    denom = v1 + v2
    gate1 = pg * v1 / denom
    gate2 = pg * v2 / denom
    e1 = grp * EXPERTS_PER_GROUP + i1.astype(I32)
    e2 = grp * EXPERTS_PER_GROUP + i2.astype(I32)

    eio = lax.broadcasted_iota(I32, (N_EXPERTS, tm), 0)
    hit1 = eio == e1
    hit2 = eio == e2
    onehot = jnp.where(hit1, 1.0, 0.0) + jnp.where(hit2, 1.0, 0.0)
    prefix = _dot(onehot.astype(BF16), tri_ref[...]) + cnt_sc[...]
    r1 = jnp.sum(jnp.where(hit1, prefix, 0.0), axis=0, keepdims=True)
    r2 = jnp.sum(jnp.where(hit2, prefix, 0.0), axis=0, keepdims=True)
    cnt = cnt_sc[...] + jnp.sum(onehot, axis=1, keepdims=True)
    cnt_sc[...] = cnt
    counts_ref[...] = jnp.broadcast_to(cnt, counts_ref.shape).astype(I32)

    row = lax.broadcasted_iota(I32, (8, tm), 0)
    route_ref[...] = jnp.where(row == 0, e1, jnp.where(row == 1, e2, jnp.where(
        row == 2, r1.astype(I32), jnp.where(row == 3, r2.astype(I32), 0))))
    gates_ref[...] = jnp.where(row == 0, gate1, jnp.where(row == 1, gate2, 0.0))


def _post_attn(x2, oa, ob, nat, wbd, wbs, wo, ln_g, wrh, wrl, tri, *, tm):
    t, d_model = x2.shape
    kern = functools.partial(_post_attn_kernel, tm=tm, d_model=d_model)
    gate_blk = NAT_GATE // d_model
    assert gate_blk * d_model == NAT_GATE
    const = lambda i: (0, 0)
    return pl.pallas_call(
        kern,
        out_shape=(
            jax.ShapeDtypeStruct((t, d_model), F32),
            jax.ShapeDtypeStruct((t, d_model), F32),
            jax.ShapeDtypeStruct((8, t), I32),
            jax.ShapeDtypeStruct((8, t), F32),
            jax.ShapeDtypeStruct((N_EXPERTS, V7X_LANES), I32),
        ),
        grid=(t // tm,),
        in_specs=[
            pl.BlockSpec((tm, d_model), lambda i: (i, 0)),
            pl.BlockSpec((tm, VA_COLS), lambda i: (i, 0)),
            pl.BlockSpec((tm, SB_COLS), lambda i: (i, 0)),
            pl.BlockSpec((tm, d_model), lambda i: (i, gate_blk)),
            pl.BlockSpec((tm, d_model), lambda i: (i, gate_blk + 1)),
            pl.BlockSpec((VA_COLS, d_model), const),
            pl.BlockSpec((SB_COLS, d_model), const),
            pl.BlockSpec((d_model, d_model), const),
            pl.BlockSpec((1, d_model), const),
            pl.BlockSpec((ROUTER_ROWS, d_model), const),
            pl.BlockSpec((ROUTER_ROWS, d_model), const),
            pl.BlockSpec((tm, tm), const),
        ],
        out_specs=(
            pl.BlockSpec((tm, d_model), lambda i: (i, 0)),
            pl.BlockSpec((tm, d_model), lambda i: (i, 0)),
            pl.BlockSpec((8, tm), lambda i: (0, i)),
            pl.BlockSpec((8, tm), lambda i: (0, i)),
            pl.BlockSpec((N_EXPERTS, V7X_LANES), const),
        ),
        scratch_shapes=[pltpu.VMEM((N_EXPERTS, 1), F32)],
        compiler_params=_cparams(("arbitrary",)),
        name="post_attn",
    )(x2, oa, ob, nat, nat, wbd, wbs, wo, ln_g, wrh, wrl, tri)


ROW_UNROLL = 8


def _row_copy(src, src_row, dst, dst_row, sem):
    return pltpu.make_async_copy(src.at[pl.ds(src_row, 1)], dst.at[pl.ds(dst_row, 1)], sem)


def _dispatch_kernel(lastblk_ref, nblk_ref, dest_ref, xn_ref, xs_ref, zero_sc, sem, zsem,
                     *, tm, tmb, nblk_max):
    @pl.when(pl.program_id(0) == 0)
    def _():
        zero_sc[...] = jnp.zeros(zero_sc.shape, F32)

        def zero_block(row):
            return pltpu.make_async_copy(
                zero_sc, xs_ref.at[pl.ds(pl.multiple_of(row, tmb), tmb)], zsem)

        for go in (lambda c: c.start(), lambda c: c.wait()):
            def seg_block(e, carry, go=go):
                @pl.when(lastblk_ref[e] >= 0)
                def _():
                    go(zero_block(lastblk_ref[e]))
                return carry

            def tail_block(b, carry, go=go):
                go(zero_block(b * tmb))
                return carry

            lax.fori_loop(0, N_EXPERTS, seg_block, 0)
            lax.fori_loop(nblk_ref[0], nblk_max, tail_block, 0)

    def start(r, carry):
        for k in range(2):
            _row_copy(xn_ref, r, xs_ref, dest_ref[k, r], sem).start()
        return carry

    lax.fori_loop(0, tm, start, 0, unroll=ROW_UNROLL)

    def wait(r, carry):
        for k in range(2):
            _row_copy(xn_ref, 0, xs_ref, 0, sem).wait()
        return carry

    lax.fori_loop(0, tm, wait, 0, unroll=ROW_UNROLL)


def _dispatch(lastblk, nblk, dest, xn, *, tmb, nblk_max, tm):
    t, d_model = xn.shape
    kern = functools.partial(_dispatch_kernel, tm=tm, tmb=tmb, nblk_max=nblk_max)
    return pl.pallas_call(
        kern,
        out_shape=jax.ShapeDtypeStruct((nblk_max * tmb, d_model), F32),
        grid_spec=pltpu.PrefetchScalarGridSpec(
            num_scalar_prefetch=2,
            grid=(t // tm,),
            in_specs=[
                pl.BlockSpec((8, tm), lambda i, *_: (0, i), memory_space=pltpu.SMEM),
                pl.BlockSpec((tm, d_model), lambda i, *_: (i, 0)),
            ],
            out_specs=pl.BlockSpec(memory_space=pl.ANY),
            scratch_shapes=[pltpu.VMEM((tmb, d_model), F32), pltpu.SemaphoreType.DMA,
                            pltpu.SemaphoreType.DMA],
        ),
        compiler_params=_cparams(("arbitrary",)),
        name="dispatch",
    )(lastblk, nblk, dest, xn)


def _combine_kernel(dest_ref, gates_ref, x1_ref, y_ref, o_ref, buf, sem, *, tm):
    def start(r, carry):
        for k in range(2):
            _row_copy(y_ref, dest_ref[k, r], buf.at[k], r, sem).start()
        return carry

    lax.fori_loop(0, tm, start, 0, unroll=ROW_UNROLL)

    def wait(r, carry):
        for k in range(2):
            _row_copy(y_ref, 0, buf.at[k], 0, sem).wait()
        return carry

    lax.fori_loop(0, tm, wait, 0, unroll=ROW_UNROLL)

    gt = jnp.transpose(gates_ref[...])
    o_ref[...] = x1_ref[...] + gt[:, 0:1] * buf[0] + gt[:, 1:2] * buf[1]


def _combine(dest, gates, x1, y, *, tm):
    t, d_model = x1.shape
    kern = functools.partial(_combine_kernel, tm=tm)
    return pl.pallas_call(
        kern,
        out_shape=jax.ShapeDtypeStruct((t, d_model), F32),
        grid=(t // tm,),
        in_specs=[
            pl.BlockSpec((8, tm), lambda i: (0, i), memory_space=pltpu.SMEM),
            pl.BlockSpec((8, tm), lambda i: (0, i)),
            pl.BlockSpec((tm, d_model), lambda i: (i, 0)),
            pl.BlockSpec(memory_space=pl.ANY),
        ],
        out_specs=pl.BlockSpec((tm, d_model), lambda i: (i, 0)),
        scratch_shapes=[pltpu.VMEM((2, tm, d_model), F32), pltpu.SemaphoreType.DMA],
        compiler_params=_cparams(("arbitrary",)),
        name="combine",
    )(dest, gates, x1, y)


def _experts_kernel(blk_e_ref, nblk_ref, xs_ref, wg_ref, wu_ref, wd_ref, y_ref,
                    wgu_sc, wd_sc, *, d_ff):
    b = pl.program_id(0)
    active = b < nblk_ref[0]

    @pl.when(active)
    def _():
        prev = blk_e_ref[jnp.maximum(b - 1, 0)]

        @pl.when(jnp.logical_or(b == 0, blk_e_ref[b] != prev))
        def _():
            wgu_sc[:, :d_ff] = wg_ref[0, 0].astype(BF16)
            wgu_sc[:, d_ff:] = wu_ref[0, 0].astype(BF16)
            wd_sc[...] = wd_ref[0, 0].astype(BF16)

        gu = _dot(xs_ref[...].astype(BF16), wgu_sc[...])
        gate = gu[:, :d_ff]
        act = (gate * jax.nn.sigmoid(gate)) * gu[:, d_ff:]
        y_ref[...] = _dot(act.astype(BF16), wd_sc[...])

    @pl.when(jnp.logical_not(active))
    def _():
        y_ref[...] = jnp.zeros(y_ref.shape, F32)


def _experts(blk_e, nblk, xs, w_gate, w_up, w_down, *, layer, tmb):
    rows, d_model = xs.shape
    d_ff = w_gate.shape[-1]
    kern = functools.partial(_experts_kernel, d_ff=d_ff)

    def x_map(b, be, nb):
        return (jnp.minimum(b, nb[0] - 1), 0)

    def w_map(b, be, nb):
        return (layer, be[b], 0, 0)

    return pl.pallas_call(
        kern,
        out_shape=jax.ShapeDtypeStruct((rows, d_model), F32),
        grid_spec=pltpu.PrefetchScalarGridSpec(
            num_scalar_prefetch=2,
            grid=(rows // tmb,),
            in_specs=[
                pl.BlockSpec((tmb, d_model), x_map),
                pl.BlockSpec((1, 1, d_model, d_ff), w_map),
                pl.BlockSpec((1, 1, d_model, d_ff), w_map),
                pl.BlockSpec((1, 1, d_ff, d_model), w_map),
            ],
            out_specs=pl.BlockSpec((tmb, d_model), lambda b, be, nb: (b, 0)),
            scratch_shapes=[pltpu.VMEM((d_model, 2 * d_ff), BF16), pltpu.VMEM((d_ff, d_model), BF16)],
        ),
        compiler_params=_cparams(("arbitrary",)),
        name="experts",
    )(blk_e, nblk, xs, w_gate, w_up, w_down)


class _Tiles:
    def __init__(self, t, s):
        self.attn = min(256, s)
        self.proj = min(512, s)
        self.rows = min(512, t)
        self.expert = min(512, t)


def _strict_upper(n):
    a = jnp.arange(n)
    return (a[:, None] < a[None, :]).astype(BF16)


def _block_table(counts, tmb, nblk_max):
    padded = ((counts + tmb - 1) // tmb) * tmb
    pend = jnp.cumsum(padded)
    pstart = (pend - padded).astype(I32)
    blk_first = jnp.arange(nblk_max, dtype=I32) * tmb
    blk_e = jnp.minimum(jnp.sum(pend[None, :] <= blk_first[:, None], axis=1), N_EXPERTS - 1).astype(I32)
    nblk = (pend[-1] // tmb).astype(I32)
    last_e = jnp.sum(jnp.where(jnp.arange(nblk_max) == nblk - 1, blk_e, 0))
    blk_e = jnp.where(jnp.arange(nblk_max) < nblk, blk_e, last_e)
    lastblk = jnp.where(padded > 0, pend - tmb, -1).astype(I32)
    return pstart, lastblk, blk_e, nblk.reshape(1)


def kernel(x, rel_bias, ln1_g, w_in, qnorm_g, knorm_g, lambda_q1, lambda_k1, lambda_q2, lambda_k2,
           subln_g, w_branch_diff, w_branch_sb, w_out, ln2_g, w_group, w_router, w_gate, w_up, w_down):
    b, s, d_model = x.shape
    depth = w_in.shape[0]
    t = b * s
    tiles = _Tiles(t, s)
    nblk_max = 2 * t // tiles.expert + N_EXPERTS

    bias = _bias_tiles_t(rel_bias, tiles.attn) * LOG2_E
    head = jnp.arange(STACK) // HEAD_DIM
    seg = jnp.where(head[:, None] == head[None, :], 1.0 / HEAD_DIM, 0.0).astype(BF16)
    tri_attn = _strict_upper(tiles.attn)
    tri_tok = _strict_upper(tiles.proj)

    c_k, c_va = 2 * STACK, QK_COLS
    c_qs = c_va + VA_COLS
    c_ks, c_vs, c_g = c_qs + SB_COLS, c_qs + 2 * SB_COLS, c_qs + 3 * SB_COLS

    x2 = x.reshape(t, d_model)
    for l in range(depth):
        lam_init = 0.8 - 0.6 * math.exp(-0.3 * l)
        w = w_in[l].astype(BF16)
        wn = jnp.concatenate([w[:, c_k:c_va], w[:, c_ks:c_vs], w[:, c_g:]], axis=1)
        wt = jnp.transpose(jnp.concatenate(
            [w[:, :c_k], w[:, c_va:c_qs], w[:, c_qs:c_ks], w[:, c_vs:c_g]], axis=1))
        n_heads = 2 * N_DIFF_HEADS
        k_gain = jnp.tile(knorm_g[l].astype(F32), n_heads).reshape(1, 2 * STACK)
        q_gain = jnp.tile(qnorm_g[l].astype(F32) * (HEAD_DIM ** -0.5 * LOG2_E), n_heads)
        q_gain = jnp.broadcast_to(q_gain[:, None], (2 * STACK, tiles.proj))
        lamv = jnp.stack([lambda_q1[l], lambda_k1[l], lambda_q2[l], lambda_k2[l]]).astype(F32)

        nat, proj_t = _in_proj(x2, ln1_g[l].reshape(1, d_model), wn, wt, k_gain, seg, q_gain,
                               batch=b, seq=s, tm=tiles.proj, blk=tiles.attn)
        nat3 = nat.reshape(b, s, nat.shape[-1])
        oa = _diff_attn(nat3, proj_t, lamv, subln_g[l].reshape(1, DIFF_V_DIM), bias,
                        tq=tiles.attn, lam_init=lam_init)
        ob = _sb_attn(nat3, proj_t, tri_attn, tq=tiles.attn)

        w_rt = jnp.concatenate([w_group[l].T, jnp.zeros((8 - N_GROUPS, d_model), F32), w_router[l].T])
        wrh, wrl = _split_bf16(w_rt.astype(F32))
        x1, xn, route, gates, counts = _post_attn(
            x2, oa.reshape(t, VA_COLS), ob.reshape(t, SB_COLS), nat,
            w_branch_diff[l].astype(BF16), w_branch_sb[l].astype(BF16), w_out[l].astype(BF16),
            ln2_g[l].reshape(1, d_model), wrh, wrl, tri_tok, tm=tiles.proj)

        pstart, lastblk, blk_e, nblk = _block_table(counts[:, 0], tiles.expert, nblk_max)
        seg_start = jnp.sum(jnp.where(route[:2, :, None] == jnp.arange(N_EXPERTS), pstart, 0), axis=-1)
        dest = jnp.concatenate([seg_start + route[2:4], jnp.zeros((6, t), I32)])
        xs = _dispatch(lastblk, nblk, dest, xn, tmb=tiles.expert, nblk_max=nblk_max, tm=tiles.rows)
        y = _experts(blk_e, nblk, xs, w_gate, w_up, w_down, layer=l, tmb=tiles.expert)
        x2 = _combine(dest, gates, x1, y, tm=tiles.rows)
    return x2.reshape(b, s, d_model)
```

```python
import functools
import math

import jax
import jax.numpy as jnp
from jax import lax
from jax.experimental import pallas as pl
from jax.experimental.pallas import tpu as pltpu

F32 = jnp.float32
BF16 = jnp.bfloat16
I32 = jnp.int32

HEAD_DIM = 64
N_DIFF_HEADS = 4
DIFF_V_DIM = 2 * HEAD_DIM
N_SB_HEADS = 8
N_BUCKETS = 32
MAX_DISTANCE = 128
N_GROUPS = 4
EXPERTS_PER_GROUP = 8
N_EXPERTS = N_GROUPS * EXPERTS_PER_GROUP
EPS = 1e-6
NEG_INF = -1e30
LOG2_E = math.log2(math.e)

V7X_LANES = 128
VMEM_LIMIT = 56 * 1024 * 1024

F32_EXP_ZERO_BELOW = -104.0

HEADS_PER_STACK = 4
STACK = HEADS_PER_STACK * HEAD_DIM

QK_COLS = 4 * N_DIFF_HEADS * HEAD_DIM
VA_COLS = N_DIFF_HEADS * DIFF_V_DIM
SB_COLS = N_SB_HEADS * HEAD_DIM
NAT_K, NAT_KS, NAT_GATE = 0, 2 * STACK, 4 * STACK
T_Q, T_VA, T_QS, T_VS = 0, 2 * STACK, 4 * STACK, 6 * STACK
T_ROWS = 8 * STACK


def _cparams(semantics, vmem=VMEM_LIMIT):
    return pltpu.CompilerParams(dimension_semantics=semantics, vmem_limit_bytes=vmem)


def _nt_dot(a, b):
    return lax.dot_general(a, b, (((1,), (1,)), ((), ())), preferred_element_type=F32)


def _dot(a, b):
    return jnp.dot(a, b, preferred_element_type=F32)


def _split_bf16(x):
    hi = x.astype(BF16)
    lo = (x - hi.astype(F32)).astype(BF16)
    return hi, lo


def _in_proj_kernel(x_ref, g_ref, wn_ref, wt_ref, kg_ref, seg_ref, qg_ref, on_ref, ot_ref,
                    *, tm, d_model, blk):
    x = x_ref[...]
    ms = jnp.mean(x * x, axis=-1, keepdims=True)
    h = ((x * lax.rsqrt(ms + EPS)) * g_ref[...]).astype(BF16)

    seg = seg_ref[...]
    for col in range(NAT_K, NAT_KS, STACK):
        acc = _dot(h, wn_ref[:, col:col + STACK])
        hi, lo = _split_bf16(acc * acc)
        msq = _dot(hi, seg) + _dot(lo, seg)
        acc = (acc * lax.rsqrt(msq + EPS)) * kg_ref[:, col:col + STACK]
        on_ref[:, col:col + STACK] = acc.astype(BF16)
    width = 2 * STACK
    on_ref[:, NAT_KS:NAT_GATE] = _dot(h, wn_ref[:, NAT_KS:NAT_GATE]).astype(BF16)
    for col in range(NAT_GATE, NAT_GATE + 2 * d_model, width):
        on_ref[:, col:col + width] = jax.nn.sigmoid(_dot(h, wn_ref[:, col:col + width])).astype(BF16)

    for row in range(0, T_ROWS, width):
        acc = _nt_dot(wt_ref[row:row + width, :], h)
        if row == T_Q:
            a3 = acc.reshape(width // HEAD_DIM, HEAD_DIM, tm)
            msq = jnp.mean(a3 * a3, axis=1, keepdims=True)
            acc = (a3 * lax.rsqrt(msq + EPS)).reshape(width, tm)
            acc = acc * qg_ref[...]
        elif row == T_QS:
            acc = acc * (HEAD_DIM ** -0.5)
        for c in range(tm // blk):
            ot_ref[0, c, row:row + width, :] = acc[:, c * blk:(c + 1) * blk].astype(BF16)


def _in_proj(x2, ln_g, wn, wt, k_gain, seg, q_gain, *, batch, seq, tm, blk):
    t, d_model = x2.shape
    nat_cols = wn.shape[1]
    tiles_per_batch = seq // tm
    kern = functools.partial(_in_proj_kernel, tm=tm, d_model=d_model, blk=blk)
    const = lambda i: (0, 0)
    return pl.pallas_call(
        kern,
        out_shape=(jax.ShapeDtypeStruct((t, nat_cols), BF16),
                   jax.ShapeDtypeStruct((batch, seq // blk, T_ROWS, blk), BF16)),
        grid=(t // tm,),
        in_specs=[
            pl.BlockSpec((tm, d_model), lambda i: (i, 0)),
            pl.BlockSpec((1, d_model), const),
            pl.BlockSpec((d_model, nat_cols), const),
            pl.BlockSpec((T_ROWS, d_model), const),
            pl.BlockSpec((1, 2 * STACK), const),
            pl.BlockSpec((STACK, STACK), const),
            pl.BlockSpec((2 * STACK, tm), const),
        ],
        out_specs=(
            pl.BlockSpec((tm, nat_cols), lambda i: (i, 0)),
            pl.BlockSpec((1, tm // blk, T_ROWS, blk),
                         lambda i: (i // tiles_per_batch, i % tiles_per_batch, 0, 0)),
        ),
        compiler_params=_cparams(("arbitrary",)),
        name="in_proj",
    )(x2, ln_g, wn, wt, k_gain, seg, q_gain)


def _stack_heads_t(q_t, qs_sc, tq):
    qf = q_t.astype(F32)
    row_head = lax.broadcasted_iota(I32, (STACK, tq), 0) // HEAD_DIM
    for h in range(HEADS_PER_STACK):
        qs_sc[:, h * tq:(h + 1) * tq] = jnp.where(row_head == h, qf, 0.0).astype(BF16)


def _t5_bucket(n):
    max_exact = N_BUCKETS // 2
    nf = jnp.maximum(n, max_exact).astype(F32)
    large = max_exact + (jnp.log(nf / max_exact) / math.log(MAX_DISTANCE / max_exact)
                         * (N_BUCKETS - max_exact)).astype(I32)
    large = jnp.minimum(large, N_BUCKETS - 1)
    return jnp.where(n < max_exact, n, large)


def _bias_tiles_t(rel_bias, tq):
    assert tq + 1 >= MAX_DISTANCE, "far blocks must lie entirely in the last bucket"
    dist = jnp.arange(tq, dtype=I32)[None, :] - jnp.arange(tq, dtype=I32)[:, None]
    rb = rel_bias.astype(F32)

    def tile(d):
        onehot = (_t5_bucket(jnp.maximum(d, 0))[..., None] == jnp.arange(N_BUCKETS)).astype(F32)
        return jnp.einsum("kqb,bh->khq", onehot, rb, precision=lax.Precision.HIGHEST)

    far = tile(jnp.full((1, 1), 2 * tq, I32))
    near = tile(dist + tq) - far
    diag = jnp.where(dist[:, None, :] >= 0, tile(dist) - far, NEG_INF)
    return jnp.stack([near, diag]).reshape(2, tq, N_DIFF_HEADS * tq)


ONES_ROWS = 16
FAR_BLOCKS = 2


def _diff_attn_kernel(lam_ref, subg_ref, bias_ref, q_ref, k_ref, v_ref, o_ref,
                      qs_sc, m_sc, acc_sc, *, tq, lam_init):
    i = pl.program_id(1)
    for mp in range(2):
        _stack_heads_t(q_ref[0, 0, mp * STACK:(mp + 1) * STACK, :], qs_sc.at[mp], tq)
    m_sc[...] = jnp.full(m_sc.shape, NEG_INF, F32)
    acc_sc[...] = jnp.zeros(acc_sc.shape, F32)
    ones = jnp.ones((ONES_ROWS, tq), BF16)

    def step(j, nb, bias_idx):
        start = pl.multiple_of(j * tq, tq)
        chains = [(mp, h) for mp in range(2) for h in range(N_DIFF_HEADS)]
        scores = []
        for mp, h in chains:
            kb = k_ref[0, pl.ds(start, nb * tq), mp * STACK:(mp + 1) * STACK]
            scores.append(_dot(kb, qs_sc[mp, :, h * tq:(h + 1) * tq]))
        probs = []
        for (mp, h), s in zip(chains, scores):
            cols = slice(h * tq, (h + 1) * tq)
            if bias_idx == "diagonal":
                s = s + bias_ref[1, :, cols]
            elif bias_idx == "near+diagonal":
                s = s + bias_ref[:, :, cols].reshape(2 * tq, tq)
            m_old = m_sc[mp, :, cols]
            m_new = jnp.maximum(m_old, jnp.max(s, axis=0, keepdims=True))
            m_sc[mp, :, cols] = m_new
            probs.append((jnp.exp2(m_old - m_new), jnp.exp2(s - m_new).astype(BF16)))
        for (mp, h), (alpha, pb) in zip(chains, probs):
            rows = slice(h * DIFF_V_DIM, (h + 1) * DIFF_V_DIM)
            pv = None
            for blk in range(nb):
                v_ext = jnp.concatenate([v_ref[0, j + blk, rows, :], ones], axis=0)
                part = _dot(v_ext, pb[blk * tq:(blk + 1) * tq])
                pv = part if pv is None else pv + part
            acc_sc[mp, h] = alpha * acc_sc[mp, h] + pv

    n_far = jnp.maximum(i - 1, 0)

    def far_body(jj, carry):
        step(FAR_BLOCKS * jj, FAR_BLOCKS, None)
        return carry

    lax.fori_loop(0, n_far // FAR_BLOCKS, far_body, 0)
    for rem in range(1, FAR_BLOCKS):
        @pl.when(n_far % FAR_BLOCKS == rem)
        def _(rem=rem):
            step(n_far - rem, rem, None)

    @pl.when(i == 0)
    def _():
        step(i, 1, "diagonal")

    @pl.when(i > 0)
    def _():
        step(i - 1, 2, "near+diagonal")

    lamv = lam_ref[...]
    lam = (jnp.exp(jnp.sum(lamv[0:1] * lamv[1:2], axis=1, keepdims=True))
           - jnp.exp(jnp.sum(lamv[2:3] * lamv[3:4], axis=1, keepdims=True)) + lam_init)
    for h in range(N_DIFF_HEADS):
        rows = slice(h * DIFF_V_DIM, (h + 1) * DIFF_V_DIM)
        a1 = acc_sc[0, h]
        a2 = acc_sc[1, h]
        inv_l1 = 1.0 / a1[DIFF_V_DIM:DIFF_V_DIM + 1]
        inv_l2 = lam / a2[DIFF_V_DIM:DIFF_V_DIM + 1]
        o_t = a1[:DIFF_V_DIM] * inv_l1 - a2[:DIFF_V_DIM] * inv_l2
        o = jnp.transpose(o_t)
        ms = jnp.mean(o * o, axis=-1, keepdims=True)
        o = ((o * lax.rsqrt(ms + EPS)) * subg_ref[...]) * (1.0 - lam_init)
        o_ref[0, :, rows] = o.astype(BF16)


def _diff_attn(nat3, proj_t, lamv, subln_g, bias, *, tq, lam_init):
    b, s, _ = nat3.shape
    nblk = s // tq
    m = N_DIFF_HEADS * tq
    kern = functools.partial(_diff_attn_kernel, tq=tq, lam_init=lam_init)
    two = 2 * STACK
    return pl.pallas_call(
        kern,
        out_shape=jax.ShapeDtypeStruct((b, s, VA_COLS), BF16),
        grid=(b, nblk),
        in_specs=[
            pl.BlockSpec((4, HEAD_DIM), lambda bi, i: (0, 0)),
            pl.BlockSpec((1, DIFF_V_DIM), lambda bi, i: (0, 0)),
            pl.BlockSpec((2, tq, m), lambda bi, i: (0, 0, 0)),
            pl.BlockSpec((1, 1, two, tq), lambda bi, i: (bi, i, T_Q // two, 0)),
            pl.BlockSpec((1, s, two), lambda bi, i: (bi, 0, NAT_K // two)),
            pl.BlockSpec((1, nblk, two, tq), lambda bi, i: (bi, 0, T_VA // two, 0)),
        ],
        out_specs=pl.BlockSpec((1, tq, VA_COLS), lambda bi, i: (bi, i, 0)),
        scratch_shapes=[
            pltpu.VMEM((2, STACK, m), BF16),
            pltpu.VMEM((2, 1, m), F32),
            pltpu.VMEM((2, N_DIFF_HEADS, DIFF_V_DIM + ONES_ROWS, tq), F32),
        ],
        compiler_params=_cparams(("arbitrary", "arbitrary")),
        name="diff_attn",
    )(lamv, subln_g, bias, proj_t, nat3, proj_t)


def _sb_attn_kernel(tri_ref, q_ref, k_ref, v_ref, o_ref, qs_sc, acc_sc, c_sc, *, tq):
    i = pl.program_id(2)
    m = HEADS_PER_STACK * tq
    _stack_heads_t(q_ref[0, 0], qs_sc, tq)
    acc_sc[...] = jnp.zeros(acc_sc.shape, F32)
    c_sc[...] = jnp.zeros(c_sc.shape, F32)

    def step(blocks):
        mask = (lax.broadcasted_iota(I32, (tq, tq), 0)
                < lax.broadcasted_iota(I32, (tq, tq), 1))
        chains = [(j, diag, h) for j, diag in blocks for h in range(HEADS_PER_STACK)]
        tri = tri_ref[...]
        zs = []
        for j, _, h in chains:
            kb = k_ref[0, pl.ds(pl.multiple_of(j * tq, tq), tq), :]
            zs.append(_dot(kb, qs_sc[:, h * tq:(h + 1) * tq]))
        parts = []
        for (j, diag, h), z in zip(chains, zs):
            nz = -z
            log_1m = jnp.minimum(nz, 0.0) - jnp.log(1.0 + jnp.exp(jnp.minimum(z, nz)))
            log_sig = z + log_1m
            if diag:
                log_1m = jnp.where(mask, log_1m, 0.0)
            parts.append((log_sig, log_1m) + _split_bf16(log_1m))
        sufs = [_dot(tri, hi) + _dot(tri, lo) for _, _, hi, lo in parts]
        c = [c_sc[:, h * tq:(h + 1) * tq] for h in range(HEADS_PER_STACK)]
        weights = []
        for (j, diag, h), (log_sig, log_1m, _, _), suf in zip(chains, parts, sufs):
            a = jnp.exp(log_sig + (suf + c[h]))
            if diag:
                a = jnp.where(mask, a, 0.0)
            c[h] = c[h] + jnp.sum(log_1m, axis=0, keepdims=True)
            weights.append(a.astype(BF16))
        for h in range(HEADS_PER_STACK):
            c_sc[:, h * tq:(h + 1) * tq] = c[h]
        for (j, _, h), a in zip(chains, weights):
            rows = slice(h * HEAD_DIM, (h + 1) * HEAD_DIM)
            acc_sc[rows, :] += _dot(v_ref[0, j, rows, :], a)

    @pl.when(i == 0)
    def _():
        step([(i, True)])

    @pl.when(i > 0)
    def _():
        step([(i, True), (i - 1, False)])

    def cond(j):
        return jnp.logical_and(j >= 0, jnp.max(c_sc[...]) > F32_EXP_ZERO_BELOW)

    def body(j):
        step([(j, False)])
        return j - 1

    lax.while_loop(cond, body, i - 2)

    o_ref[0] = jnp.transpose(acc_sc[...]).astype(BF16)


def _sb_attn(nat3, proj_t, tri, *, tq):
    b, s, _ = nat3.shape
    nblk = s // tq
    n_stacks = SB_COLS // STACK
    m = HEADS_PER_STACK * tq
    kern = functools.partial(_sb_attn_kernel, tq=tq)
    return pl.pallas_call(
        kern,
        out_shape=jax.ShapeDtypeStruct((b, s, SB_COLS), BF16),
        grid=(b, n_stacks, nblk),
        in_specs=[
            pl.BlockSpec((tq, tq), lambda bi, g, i: (0, 0)),
            pl.BlockSpec((1, 1, STACK, tq), lambda bi, g, i: (bi, i, T_QS // STACK + g, 0)),
            pl.BlockSpec((1, s, STACK), lambda bi, g, i: (bi, 0, NAT_KS // STACK + g)),
            pl.BlockSpec((1, nblk, STACK, tq), lambda bi, g, i: (bi, 0, T_VS // STACK + g, 0)),
        ],
        out_specs=pl.BlockSpec((1, tq, STACK), lambda bi, g, i: (bi, i, g)),
        scratch_shapes=[pltpu.VMEM((STACK, m), BF16), pltpu.VMEM((STACK, tq), F32),
                        pltpu.VMEM((1, m), F32)],
        compiler_params=_cparams(("arbitrary", "arbitrary", "arbitrary")),
        name="sb_attn",
    )(tri, proj_t, nat3, proj_t)


ROUTER_ROWS = 8 + N_EXPERTS


def _post_attn_kernel(x_ref, oa_ref, ob_ref, ga_ref, gb_ref, wbd_ref, wbs_ref, wo_ref, g_ref,
                      wrh_ref, wrl_ref, tri_ref,
                      x1_ref, xn_ref, route_ref, gates_ref, counts_ref, cnt_sc, *, tm, d_model):
    @pl.when(pl.program_id(0) == 0)
    def _():
        cnt_sc[...] = jnp.zeros(cnt_sc.shape, F32)

    ga = ga_ref[...].astype(F32)
    gb = gb_ref[...].astype(F32)
    mixed = ga * _dot(oa_ref[...], wbd_ref[...]) + gb * _dot(ob_ref[...], wbs_ref[...])
    x1 = x_ref[...] + _dot(mixed.astype(BF16), wo_ref[...])
    x1_ref[...] = x1
    ms = jnp.mean(x1 * x1, axis=-1, keepdims=True)
    xn = (x1 * lax.rsqrt(ms + EPS)) * g_ref[...]
    xn_ref[...] = xn

    xh, xl = _split_bf16(xn)
    wrh = wrh_ref[...]
    logits = _nt_dot(wrh, xh) + _nt_dot(wrh, xl) + _nt_dot(wrl_ref[...], xh)

    gl = [logits[r:r + 1] for r in range(N_GROUPS)]
    gmax = functools.reduce(jnp.maximum, gl)
    grp = jnp.full((1, tm), N_GROUPS - 1, I32)
    for r in range(N_GROUPS - 2, -1, -1):
        grp = jnp.where(gl[r] == gmax, r, grp)
    pg = 1.0 / functools.reduce(lambda a, b: a + b, [jnp.exp(v - gmax) for v in gl])

    el = logits[8:8 + EXPERTS_PER_GROUP]
    for r in range(1, N_GROUPS):
        lo = 8 + r * EXPERTS_PER_GROUP
        el = jnp.where(grp == r, logits[lo:lo + EXPERTS_PER_GROUP], el)
    ex = jnp.exp(el - jnp.max(el, axis=0, keepdims=True))
    prob = ex / jnp.sum(ex, axis=0, keepdims=True)
    sub = lax.broadcasted_iota(I32, (EXPERTS_PER_GROUP, tm), 0).astype(F32)
    none = float(EXPERTS_PER_GROUP)
    v1 = jnp.max(prob, axis=0, keepdims=True)
    i1 = jnp.min(jnp.where(prob == v1, sub, none), axis=0, keepdims=True)
    rest = jnp.where(sub == i1, -1.0, prob)
    v2 = jnp.max(rest, axis=0, keepdims=True)
    i2 = jnp.min(jnp.where(rest == v2, sub, none), axis=0, keepdims=True)
    denom = v1 + v2
    gate1 = pg * v1 / denom
    gate2 = pg * v2 / denom
    e1 = grp * EXPERTS_PER_GROUP + i1.astype(I32)
    e2 = grp * EXPERTS_PER_GROUP + i2.astype(I32)

    eio = lax.broadcasted_iota(I32, (N_EXPERTS, tm), 0)
    hit1 = eio == e1
    hit2 = eio == e2
    onehot = jnp.where(hit1, 1.0, 0.0) + jnp.where(hit2, 1.0, 0.0)
    prefix = _dot(onehot.astype(BF16), tri_ref[...]) + cnt_sc[...]
    r1 = jnp.sum(jnp.where(hit1, prefix, 0.0), axis=0, keepdims=True)
    r2 = jnp.sum(jnp.where(hit2, prefix, 0.0), axis=0, keepdims=True)
    cnt = cnt_sc[...] + jnp.sum(onehot, axis=1, keepdims=True)
    cnt_sc[...] = cnt
    counts_ref[...] = jnp.broadcast_to(cnt, counts_ref.shape).astype(I32)

    row = lax.broadcasted_iota(I32, (8, tm), 0)
    route_ref[...] = jnp.where(row == 0, e1, jnp.where(row == 1, e2, jnp.where(
        row == 2, r1.astype(I32), jnp.where(row == 3, r2.astype(I32), 0))))
    gates_ref[...] = jnp.where(row == 0, gate1, jnp.where(row == 1, gate2, 0.0))


def _post_attn(x2, oa, ob, nat, wbd, wbs, wo, ln_g, wrh, wrl, tri, *, tm):
    t, d_model = x2.shape
    kern = functools.partial(_post_attn_kernel, tm=tm, d_model=d_model)
    gate_blk = NAT_GATE // d_model
    assert gate_blk * d_model == NAT_GATE
    const = lambda i: (0, 0)
    return pl.pallas_call(
        kern,
        out_shape=(
            jax.ShapeDtypeStruct((t, d_model), F32),
            jax.ShapeDtypeStruct((t, d_model), F32),
            jax.ShapeDtypeStruct((8, t), I32),
            jax.ShapeDtypeStruct((8, t), F32),
            jax.ShapeDtypeStruct((N_EXPERTS, V7X_LANES), I32),
        ),
        grid=(t // tm,),
        in_specs=[
            pl.BlockSpec((tm, d_model), lambda i: (i, 0)),
            pl.BlockSpec((tm, VA_COLS), lambda i: (i, 0)),
            pl.BlockSpec((tm, SB_COLS), lambda i: (i, 0)),
            pl.BlockSpec((tm, d_model), lambda i: (i, gate_blk)),
            pl.BlockSpec((tm, d_model), lambda i: (i, gate_blk + 1)),
            pl.BlockSpec((VA_COLS, d_model), const),
            pl.BlockSpec((SB_COLS, d_model), const),
            pl.BlockSpec((d_model, d_model), const),
            pl.BlockSpec((1, d_model), const),
            pl.BlockSpec((ROUTER_ROWS, d_model), const),
            pl.BlockSpec((ROUTER_ROWS, d_model), const),
            pl.BlockSpec((tm, tm), const),
        ],
        out_specs=(
            pl.BlockSpec((tm, d_model), lambda i: (i, 0)),
            pl.BlockSpec((tm, d_model), lambda i: (i, 0)),
            pl.BlockSpec((8, tm), lambda i: (0, i)),
            pl.BlockSpec((8, tm), lambda i: (0, i)),
            pl.BlockSpec((N_EXPERTS, V7X_LANES), const),
        ),
        scratch_shapes=[pltpu.VMEM((N_EXPERTS, 1), F32)],
        compiler_params=_cparams(("arbitrary",)),
        name="post_attn",
    )(x2, oa, ob, nat, nat, wbd, wbs, wo, ln_g, wrh, wrl, tri)


ROW_UNROLL = 8


def _row_copy(src, src_row, dst, dst_row, sem):
    return pltpu.make_async_copy(src.at[pl.ds(src_row, 1)], dst.at[pl.ds(dst_row, 1)], sem)


def _dispatch_kernel(lastblk_ref, nblk_ref, dest_ref, xn_ref, xs_ref, zero_sc, sem, zsem,
                     *, tm, tmb, nblk_max):
    @pl.when(pl.program_id(0) == 0)
    def _():
        zero_sc[...] = jnp.zeros(zero_sc.shape, F32)

        def zero_block(row):
            return pltpu.make_async_copy(
                zero_sc, xs_ref.at[pl.ds(pl.multiple_of(row, tmb), tmb)], zsem)

        for go in (lambda c: c.start(), lambda c: c.wait()):
            def seg_block(e, carry, go=go):
                @pl.when(lastblk_ref[e] >= 0)
                def _():
                    go(zero_block(lastblk_ref[e]))
                return carry

            def tail_block(b, carry, go=go):
                go(zero_block(b * tmb))
                return carry

            lax.fori_loop(0, N_EXPERTS, seg_block, 0)
            lax.fori_loop(nblk_ref[0], nblk_max, tail_block, 0)

    for r in range(tm):
        for k in range(2):
            _row_copy(xn_ref, r, xs_ref, dest_ref[k, r], sem).start()

    def wait(r, carry):
        for k in range(2):
            _row_copy(xn_ref, 0, xs_ref, 0, sem).wait()
        return carry

    lax.fori_loop(0, tm, wait, 0, unroll=ROW_UNROLL)


def _dispatch(lastblk, nblk, dest, xn, *, tmb, nblk_max, tm):
    t, d_model = xn.shape
    kern = functools.partial(_dispatch_kernel, tm=tm, tmb=tmb, nblk_max=nblk_max)
    return pl.pallas_call(
        kern,
        out_shape=jax.ShapeDtypeStruct((nblk_max * tmb, d_model), F32),
        grid_spec=pltpu.PrefetchScalarGridSpec(
            num_scalar_prefetch=2,
            grid=(t // tm,),
            in_specs=[
                pl.BlockSpec((8, tm), lambda i, *_: (0, i), memory_space=pltpu.SMEM),
                pl.BlockSpec((tm, d_model), lambda i, *_: (i, 0)),
            ],
            out_specs=pl.BlockSpec(memory_space=pl.ANY),
            scratch_shapes=[pltpu.VMEM((tmb, d_model), F32), pltpu.SemaphoreType.DMA,
                            pltpu.SemaphoreType.DMA],
        ),
        compiler_params=_cparams(("arbitrary",)),
        name="dispatch",
    )(lastblk, nblk, dest, xn)


def _combine_kernel(dest_ref, gates_ref, x1_ref, y_ref, o_ref, buf, sem, *, tm):
    for r in range(tm):
        for k in range(2):
            _row_copy(y_ref, dest_ref[k, r], buf.at[k], r, sem).start()

    def wait(r, carry):
        for k in range(2):
            _row_copy(y_ref, 0, buf.at[k], 0, sem).wait()
        return carry

    lax.fori_loop(0, tm, wait, 0, unroll=ROW_UNROLL)

    gt = jnp.transpose(gates_ref[...])
    o_ref[...] = x1_ref[...] + gt[:, 0:1] * buf[0] + gt[:, 1:2] * buf[1]


def _combine(dest, gates, x1, y, *, tm):
    t, d_model = x1.shape
    kern = functools.partial(_combine_kernel, tm=tm)
    return pl.pallas_call(
        kern,
        out_shape=jax.ShapeDtypeStruct((t, d_model), F32),
        grid=(t // tm,),
        in_specs=[
            pl.BlockSpec((8, tm), lambda i: (0, i), memory_space=pltpu.SMEM),
            pl.BlockSpec((8, tm), lambda i: (0, i)),
            pl.BlockSpec((tm, d_model), lambda i: (i, 0)),
            pl.BlockSpec(memory_space=pl.ANY),
        ],
        out_specs=pl.BlockSpec((tm, d_model), lambda i: (i, 0)),
        scratch_shapes=[pltpu.VMEM((2, tm, d_model), F32), pltpu.SemaphoreType.DMA],
        compiler_params=_cparams(("arbitrary",)),
        name="combine",
    )(dest, gates, x1, y)


def _experts_kernel(blk_e_ref, nblk_ref, xs_ref, wg_ref, wu_ref, wd_ref, y_ref,
                    wgu_sc, wd_sc, *, d_ff):
    b = pl.program_id(0)
    active = b < nblk_ref[0]

    @pl.when(active)
    def _():
        prev = blk_e_ref[jnp.maximum(b - 1, 0)]

        @pl.when(jnp.logical_or(b == 0, blk_e_ref[b] != prev))
        def _():
            wgu_sc[:, :d_ff] = wg_ref[0, 0].astype(BF16)
            wgu_sc[:, d_ff:] = wu_ref[0, 0].astype(BF16)
            wd_sc[...] = wd_ref[0, 0].astype(BF16)

        gu = _dot(xs_ref[...].astype(BF16), wgu_sc[...])
        gate = gu[:, :d_ff]
        act = (gate * jax.nn.sigmoid(gate)) * gu[:, d_ff:]
        y_ref[...] = _dot(act.astype(BF16), wd_sc[...])

    @pl.when(jnp.logical_not(active))
    def _():
        y_ref[...] = jnp.zeros(y_ref.shape, F32)


def _experts(blk_e, nblk, xs, w_gate, w_up, w_down, *, layer, tmb):
    rows, d_model = xs.shape
    d_ff = w_gate.shape[-1]
    kern = functools.partial(_experts_kernel, d_ff=d_ff)

    def x_map(b, be, nb):
        return (jnp.minimum(b, nb[0] - 1), 0)

    def w_map(b, be, nb):
        return (layer, be[b], 0, 0)

    return pl.pallas_call(
        kern,
        out_shape=jax.ShapeDtypeStruct((rows, d_model), F32),
        grid_spec=pltpu.PrefetchScalarGridSpec(
            num_scalar_prefetch=2,
            grid=(rows // tmb,),
            in_specs=[
                pl.BlockSpec((tmb, d_model), x_map),
                pl.BlockSpec((1, 1, d_model, d_ff), w_map),
                pl.BlockSpec((1, 1, d_model, d_ff), w_map),
                pl.BlockSpec((1, 1, d_ff, d_model), w_map),
            ],
            out_specs=pl.BlockSpec((tmb, d_model), lambda b, be, nb: (b, 0)),
            scratch_shapes=[pltpu.VMEM((d_model, 2 * d_ff), BF16), pltpu.VMEM((d_ff, d_model), BF16)],
        ),
        compiler_params=_cparams(("arbitrary",)),
        name="experts",
    )(blk_e, nblk, xs, w_gate, w_up, w_down)


class _Tiles:
    def __init__(self, t, s):
        self.attn = min(256, s)
        self.proj = min(512, s)
        self.rows = min(256, t)
        self.expert = min(512, t)


def _strict_upper(n):
    a = jnp.arange(n)
    return (a[:, None] < a[None, :]).astype(BF16)


def _block_table(counts, tmb, nblk_max):
    padded = ((counts + tmb - 1) // tmb) * tmb
    pend = jnp.cumsum(padded)
    pstart = (pend - padded).astype(I32)
    blk_first = jnp.arange(nblk_max, dtype=I32) * tmb
    blk_e = jnp.minimum(jnp.sum(pend[None, :] <= blk_first[:, None], axis=1), N_EXPERTS - 1).astype(I32)
    nblk = (pend[-1] // tmb).astype(I32)
    last_e = jnp.sum(jnp.where(jnp.arange(nblk_max) == nblk - 1, blk_e, 0))
    blk_e = jnp.where(jnp.arange(nblk_max) < nblk, blk_e, last_e)
    lastblk = jnp.where(padded > 0, pend - tmb, -1).astype(I32)
    return pstart, lastblk, blk_e, nblk.reshape(1)


def kernel(x, rel_bias, ln1_g, w_in, qnorm_g, knorm_g, lambda_q1, lambda_k1, lambda_q2, lambda_k2,
           subln_g, w_branch_diff, w_branch_sb, w_out, ln2_g, w_group, w_router, w_gate, w_up, w_down):
    b, s, d_model = x.shape
    depth = w_in.shape[0]
    t = b * s
    tiles = _Tiles(t, s)
    nblk_max = 2 * t // tiles.expert + N_EXPERTS

    bias = _bias_tiles_t(rel_bias, tiles.attn) * LOG2_E
    head = jnp.arange(STACK) // HEAD_DIM
    seg = jnp.where(head[:, None] == head[None, :], 1.0 / HEAD_DIM, 0.0).astype(BF16)
    tri_attn = _strict_upper(tiles.attn)
    tri_tok = _strict_upper(tiles.proj)

    c_k, c_va = 2 * STACK, QK_COLS
    c_qs = c_va + VA_COLS
    c_ks, c_vs, c_g = c_qs + SB_COLS, c_qs + 2 * SB_COLS, c_qs + 3 * SB_COLS

    x2 = x.reshape(t, d_model)
    for l in range(depth):
        lam_init = 0.8 - 0.6 * math.exp(-0.3 * l)
        w = w_in[l].astype(BF16)
        wn = jnp.concatenate([w[:, c_k:c_va], w[:, c_ks:c_vs], w[:, c_g:]], axis=1)
        wt = jnp.transpose(jnp.concatenate(
            [w[:, :c_k], w[:, c_va:c_qs], w[:, c_qs:c_ks], w[:, c_vs:c_g]], axis=1))
        n_heads = 2 * N_DIFF_HEADS
        k_gain = jnp.tile(knorm_g[l].astype(F32), n_heads).reshape(1, 2 * STACK)
        q_gain = jnp.tile(qnorm_g[l].astype(F32) * (HEAD_DIM ** -0.5 * LOG2_E), n_heads)
        q_gain = jnp.broadcast_to(q_gain[:, None], (2 * STACK, tiles.proj))
        lamv = jnp.stack([lambda_q1[l], lambda_k1[l], lambda_q2[l], lambda_k2[l]]).astype(F32)

        nat, proj_t = _in_proj(x2, ln1_g[l].reshape(1, d_model), wn, wt, k_gain, seg, q_gain,
                               batch=b, seq=s, tm=tiles.proj, blk=tiles.attn)
        nat3 = nat.reshape(b, s, nat.shape[-1])
        oa = _diff_attn(nat3, proj_t, lamv, subln_g[l].reshape(1, DIFF_V_DIM), bias,
                        tq=tiles.attn, lam_init=lam_init)
        ob = _sb_attn(nat3, proj_t, tri_attn, tq=tiles.attn)

        w_rt = jnp.concatenate([w_group[l].T, jnp.zeros((8 - N_GROUPS, d_model), F32), w_router[l].T])
        wrh, wrl = _split_bf16(w_rt.astype(F32))
        x1, xn, route, gates, counts = _post_attn(
            x2, oa.reshape(t, VA_COLS), ob.reshape(t, SB_COLS), nat,
            w_branch_diff[l].astype(BF16), w_branch_sb[l].astype(BF16), w_out[l].astype(BF16),
            ln2_g[l].reshape(1, d_model), wrh, wrl, tri_tok, tm=tiles.proj)

        pstart, lastblk, blk_e, nblk = _block_table(counts[:, 0], tiles.expert, nblk_max)
        seg_start = jnp.sum(jnp.where(route[:2, :, None] == jnp.arange(N_EXPERTS), pstart, 0), axis=-1)
        dest = jnp.concatenate([seg_start + route[2:4], jnp.zeros((6, t), I32)])
        xs = _dispatch(lastblk, nblk, dest, xn, tmb=tiles.expert, nblk_max=nblk_max, tm=tiles.rows)
        y = _experts(blk_e, nblk, xs, w_gate, w_up, w_down, layer=l, tmb=tiles.expert)
        x2 = _combine(dest, gates, x1, y, tm=tiles.rows)
    return x2.reshape(b, s, d_model)
```

```python
import functools
import math

import jax
import jax.numpy as jnp
from jax import lax
from jax.experimental import pallas as pl
from jax.experimental.pallas import tpu as pltpu

F32 = jnp.float32
BF16 = jnp.bfloat16
I32 = jnp.int32

HEAD_DIM = 64
N_DIFF_HEADS = 4
DIFF_V_DIM = 2 * HEAD_DIM
N_SB_HEADS = 8
N_BUCKETS = 32
MAX_DISTANCE = 128
N_GROUPS = 4
EXPERTS_PER_GROUP = 8
N_EXPERTS = N_GROUPS * EXPERTS_PER_GROUP
EPS = 1e-6
NEG_INF = -1e30
LOG2_E = math.log2(math.e)

V7X_LANES = 128
VMEM_LIMIT = 56 * 1024 * 1024

F32_EXP_ZERO_BELOW = -104.0

HEADS_PER_STACK = 4
STACK = HEADS_PER_STACK * HEAD_DIM

QK_COLS = 4 * N_DIFF_HEADS * HEAD_DIM
VA_COLS = N_DIFF_HEADS * DIFF_V_DIM
SB_COLS = N_SB_HEADS * HEAD_DIM
NAT_K, NAT_KS, NAT_GATE = 0, 2 * STACK, 4 * STACK
T_Q, T_VA, T_QS, T_VS = 0, 2 * STACK, 4 * STACK, 6 * STACK
T_ROWS = 8 * STACK


def _cparams(semantics, vmem=VMEM_LIMIT):
    return pltpu.CompilerParams(dimension_semantics=semantics, vmem_limit_bytes=vmem)


def _nt_dot(a, b):
    return lax.dot_general(a, b, (((1,), (1,)), ((), ())), preferred_element_type=F32)


def _dot(a, b):
    return jnp.dot(a, b, preferred_element_type=F32)


def _split_bf16(x):
    hi = x.astype(BF16)
    lo = (x - hi.astype(F32)).astype(BF16)
    return hi, lo


def _in_proj_kernel(x_ref, g_ref, wn_ref, wt_ref, kg_ref, seg_ref, qg_ref, on_ref, ot_ref,
                    *, tm, d_model, blk):
    x = x_ref[...]
    ms = jnp.mean(x * x, axis=-1, keepdims=True)
    h = ((x * lax.rsqrt(ms + EPS)) * g_ref[...]).astype(BF16)

    seg = seg_ref[...]
    for col in range(NAT_K, NAT_KS, STACK):
        acc = _dot(h, wn_ref[:, col:col + STACK])
        hi, lo = _split_bf16(acc * acc)
        msq = _dot(hi, seg) + _dot(lo, seg)
        acc = (acc * lax.rsqrt(msq + EPS)) * kg_ref[:, col:col + STACK]
        on_ref[:, col:col + STACK] = acc.astype(BF16)
    width = 2 * STACK
    on_ref[:, NAT_KS:NAT_GATE] = _dot(h, wn_ref[:, NAT_KS:NAT_GATE]).astype(BF16)
    for col in range(NAT_GATE, NAT_GATE + 2 * d_model, width):
        on_ref[:, col:col + width] = jax.nn.sigmoid(_dot(h, wn_ref[:, col:col + width])).astype(BF16)

    for row in range(0, T_ROWS, width):
        acc = _nt_dot(wt_ref[row:row + width, :], h)
        if row == T_Q:
            a3 = acc.reshape(width // HEAD_DIM, HEAD_DIM, tm)
            msq = jnp.mean(a3 * a3, axis=1, keepdims=True)
            acc = (a3 * lax.rsqrt(msq + EPS)).reshape(width, tm)
            acc = acc * qg_ref[...]
        elif row == T_QS:
            acc = acc * (HEAD_DIM ** -0.5)
        for c in range(tm // blk):
            ot_ref[0, c, row:row + width, :] = acc[:, c * blk:(c + 1) * blk].astype(BF16)


def _in_proj(x2, ln_g, wn, wt, k_gain, seg, q_gain, *, batch, seq, tm, blk):
    t, d_model = x2.shape
    nat_cols = wn.shape[1]
    tiles_per_batch = seq // tm
    kern = functools.partial(_in_proj_kernel, tm=tm, d_model=d_model, blk=blk)
    const = lambda i: (0, 0)
    return pl.pallas_call(
        kern,
        out_shape=(jax.ShapeDtypeStruct((t, nat_cols), BF16),
                   jax.ShapeDtypeStruct((batch, seq // blk, T_ROWS, blk), BF16)),
        grid=(t // tm,),
        in_specs=[
            pl.BlockSpec((tm, d_model), lambda i: (i, 0)),
            pl.BlockSpec((1, d_model), const),
            pl.BlockSpec((d_model, nat_cols), const),
            pl.BlockSpec((T_ROWS, d_model), const),
            pl.BlockSpec((1, 2 * STACK), const),
            pl.BlockSpec((STACK, STACK), const),
            pl.BlockSpec((2 * STACK, tm), const),
        ],
        out_specs=(
            pl.BlockSpec((tm, nat_cols), lambda i: (i, 0)),
            pl.BlockSpec((1, tm // blk, T_ROWS, blk),
                         lambda i: (i // tiles_per_batch, i % tiles_per_batch, 0, 0)),
        ),
        compiler_params=_cparams(("arbitrary",)),
        name="in_proj",
    )(x2, ln_g, wn, wt, k_gain, seg, q_gain)


def _stack_heads_t(q_t, qs_sc, tq):
    qf = q_t.astype(F32)
    row_head = lax.broadcasted_iota(I32, (STACK, tq), 0) // HEAD_DIM
    for h in range(HEADS_PER_STACK):
        qs_sc[:, h * tq:(h + 1) * tq] = jnp.where(row_head == h, qf, 0.0).astype(BF16)


def _t5_bucket(n):
    max_exact = N_BUCKETS // 2
    nf = jnp.maximum(n, max_exact).astype(F32)
    large = max_exact + (jnp.log(nf / max_exact) / math.log(MAX_DISTANCE / max_exact)
                         * (N_BUCKETS - max_exact)).astype(I32)
    large = jnp.minimum(large, N_BUCKETS - 1)
    return jnp.where(n < max_exact, n, large)


def _bias_tiles_t(rel_bias, tq):
    assert tq + 1 >= MAX_DISTANCE, "far blocks must lie entirely in the last bucket"
    dist = jnp.arange(tq, dtype=I32)[None, :] - jnp.arange(tq, dtype=I32)[:, None]
    rb = rel_bias.astype(F32)

    def tile(d):
        onehot = (_t5_bucket(jnp.maximum(d, 0))[..., None] == jnp.arange(N_BUCKETS)).astype(F32)
        return jnp.einsum("kqb,bh->khq", onehot, rb, precision=lax.Precision.HIGHEST)

    far = tile(jnp.full((1, 1), 2 * tq, I32))
    near = tile(dist + tq) - far
    diag = jnp.where(dist[:, None, :] >= 0, tile(dist) - far, NEG_INF)
    return jnp.stack([near, diag]).reshape(2, tq, N_DIFF_HEADS * tq)


ONES_ROWS = 16
FAR_BLOCKS = 2


def _diff_attn_kernel(lam_ref, subg_ref, bias_ref, q_ref, k_ref, v_ref, o_ref,
                      qs_sc, m_sc, acc_sc, *, tq, lam_init):
    i = pl.program_id(1)
    for mp in range(2):
        _stack_heads_t(q_ref[0, 0, mp * STACK:(mp + 1) * STACK, :], qs_sc.at[mp], tq)
    m_sc[...] = jnp.full(m_sc.shape, NEG_INF, F32)
    acc_sc[...] = jnp.zeros(acc_sc.shape, F32)
    ones = jnp.ones((ONES_ROWS, tq), BF16)

    def step(j, nb, bias_idx):
        start = pl.multiple_of(j * tq, tq)
        chains = [(mp, h) for mp in range(2) for h in range(N_DIFF_HEADS)]
        scores = []
        for mp, h in chains:
            kb = k_ref[0, pl.ds(start, nb * tq), mp * STACK:(mp + 1) * STACK]
            scores.append(_dot(kb, qs_sc[mp, :, h * tq:(h + 1) * tq]))
        probs = []
        for (mp, h), s in zip(chains, scores):
            cols = slice(h * tq, (h + 1) * tq)
            if bias_idx == "diagonal":
                s = s + bias_ref[1, :, cols]
            elif bias_idx == "near+diagonal":
                s = s + bias_ref[:, :, cols].reshape(2 * tq, tq)
            m_old = m_sc[mp, :, cols]
            m_new = jnp.maximum(m_old, jnp.max(s, axis=0, keepdims=True))
            m_sc[mp, :, cols] = m_new
            probs.append((jnp.exp2(m_old - m_new), jnp.exp2(s - m_new).astype(BF16)))
        for (mp, h), (alpha, pb) in zip(chains, probs):
            rows = slice(h * DIFF_V_DIM, (h + 1) * DIFF_V_DIM)
            pv = None
            for blk in range(nb):
                v_ext = jnp.concatenate([v_ref[0, j + blk, rows, :], ones], axis=0)
                part = _dot(v_ext, pb[blk * tq:(blk + 1) * tq])
                pv = part if pv is None else pv + part
            acc_sc[mp, h] = alpha * acc_sc[mp, h] + pv

    n_far = jnp.maximum(i - 1, 0)

    def far_body(jj, carry):
        step(FAR_BLOCKS * jj, FAR_BLOCKS, None)
        return carry

    lax.fori_loop(0, n_far // FAR_BLOCKS, far_body, 0)
    for rem in range(1, FAR_BLOCKS):
        @pl.when(n_far % FAR_BLOCKS == rem)
        def _(rem=rem):
            step(n_far - rem, rem, None)

    @pl.when(i == 0)
    def _():
        step(i, 1, "diagonal")

    @pl.when(i > 0)
    def _():
        step(i - 1, 2, "near+diagonal")

    lamv = lam_ref[...]
    lam = (jnp.exp(jnp.sum(lamv[0:1] * lamv[1:2], axis=1, keepdims=True))
           - jnp.exp(jnp.sum(lamv[2:3] * lamv[3:4], axis=1, keepdims=True)) + lam_init)
    for h in range(N_DIFF_HEADS):
        rows = slice(h * DIFF_V_DIM, (h + 1) * DIFF_V_DIM)
        a1 = acc_sc[0, h]
        a2 = acc_sc[1, h]
        inv_l1 = 1.0 / a1[DIFF_V_DIM:DIFF_V_DIM + 1]
        inv_l2 = lam / a2[DIFF_V_DIM:DIFF_V_DIM + 1]
        o_t = a1[:DIFF_V_DIM] * inv_l1 - a2[:DIFF_V_DIM] * inv_l2
        o = jnp.transpose(o_t)
        ms = jnp.mean(o * o, axis=-1, keepdims=True)
        o = ((o * lax.rsqrt(ms + EPS)) * subg_ref[...]) * (1.0 - lam_init)
        o_ref[0, :, rows] = o.astype(BF16)


def _diff_attn(nat3, proj_t, lamv, subln_g, bias, *, tq, lam_init):
    b, s, _ = nat3.shape
    nblk = s // tq
    m = N_DIFF_HEADS * tq
    kern = functools.partial(_diff_attn_kernel, tq=tq, lam_init=lam_init)
    two = 2 * STACK
    return pl.pallas_call(
        kern,
        out_shape=jax.ShapeDtypeStruct((b, s, VA_COLS), BF16),
        grid=(b, nblk),
        in_specs=[
            pl.BlockSpec((4, HEAD_DIM), lambda bi, i: (0, 0)),
            pl.BlockSpec((1, DIFF_V_DIM), lambda bi, i: (0, 0)),
            pl.BlockSpec((2, tq, m), lambda bi, i: (0, 0, 0)),
            pl.BlockSpec((1, 1, two, tq), lambda bi, i: (bi, i, T_Q // two, 0)),
            pl.BlockSpec((1, s, two), lambda bi, i: (bi, 0, NAT_K // two)),
            pl.BlockSpec((1, nblk, two, tq), lambda bi, i: (bi, 0, T_VA // two, 0)),
        ],
        out_specs=pl.BlockSpec((1, tq, VA_COLS), lambda bi, i: (bi, i, 0)),
        scratch_shapes=[
            pltpu.VMEM((2, STACK, m), BF16),
            pltpu.VMEM((2, 1, m), F32),
            pltpu.VMEM((2, N_DIFF_HEADS, DIFF_V_DIM + ONES_ROWS, tq), F32),
        ],
        compiler_params=_cparams(("arbitrary", "arbitrary")),
        name="diff_attn",
    )(lamv, subln_g, bias, proj_t, nat3, proj_t)


def _sb_attn_kernel(tri_ref, q_ref, k_ref, v_ref, o_ref, qs_sc, acc_sc, c_sc, *, tq):
    i = pl.program_id(2)
    m = HEADS_PER_STACK * tq
    _stack_heads_t(q_ref[0, 0], qs_sc, tq)
    acc_sc[...] = jnp.zeros(acc_sc.shape, F32)
    c_sc[...] = jnp.zeros(c_sc.shape, F32)

    def step(blocks):
        mask = (lax.broadcasted_iota(I32, (tq, tq), 0)
                < lax.broadcasted_iota(I32, (tq, tq), 1))
        chains = [(j, diag, h) for j, diag in blocks for h in range(HEADS_PER_STACK)]
        tri = tri_ref[...]
        zs = []
        for j, _, h in chains:
            kb = k_ref[0, pl.ds(pl.multiple_of(j * tq, tq), tq), :]
            zs.append(_dot(kb, qs_sc[:, h * tq:(h + 1) * tq]))
        parts = []
        for (j, diag, h), z in zip(chains, zs):
            nz = -z
            log_1m = jnp.minimum(nz, 0.0) - jnp.log(1.0 + jnp.exp(jnp.minimum(z, nz)))
            log_sig = z + log_1m
            if diag:
                log_1m = jnp.where(mask, log_1m, 0.0)
            parts.append((log_sig, log_1m) + _split_bf16(log_1m))
        sufs = [_dot(tri, hi) + _dot(tri, lo) for _, _, hi, lo in parts]
        c = [c_sc[:, h * tq:(h + 1) * tq] for h in range(HEADS_PER_STACK)]
        weights = []
        for (j, diag, h), (log_sig, log_1m, _, _), suf in zip(chains, parts, sufs):
            a = jnp.exp(log_sig + (suf + c[h]))
            if diag:
                a = jnp.where(mask, a, 0.0)
            c[h] = c[h] + jnp.sum(log_1m, axis=0, keepdims=True)
            weights.append(a.astype(BF16))
        for h in range(HEADS_PER_STACK):
            c_sc[:, h * tq:(h + 1) * tq] = c[h]
        for (j, _, h), a in zip(chains, weights):
            rows = slice(h * HEAD_DIM, (h + 1) * HEAD_DIM)
            acc_sc[rows, :] += _dot(v_ref[0, j, rows, :], a)

    @pl.when(i == 0)
    def _():
        step([(i, True)])

    @pl.when(i > 0)
    def _():
        step([(i, True), (i - 1, False)])

    def cond(j):
        return jnp.logical_and(j >= 0, jnp.max(c_sc[...]) > F32_EXP_ZERO_BELOW)

    def body(j):
        step([(j, False)])
        return j - 1

    lax.while_loop(cond, body, i - 2)

    o_ref[0] = jnp.transpose(acc_sc[...]).astype(BF16)


def _sb_attn(nat3, proj_t, tri, *, tq):
    b, s, _ = nat3.shape
    nblk = s // tq
    n_stacks = SB_COLS // STACK
    m = HEADS_PER_STACK * tq
    kern = functools.partial(_sb_attn_kernel, tq=tq)
    return pl.pallas_call(
        kern,
        out_shape=jax.ShapeDtypeStruct((b, s, SB_COLS), BF16),
        grid=(b, n_stacks, nblk),
        in_specs=[
            pl.BlockSpec((tq, tq), lambda bi, g, i: (0, 0)),
            pl.BlockSpec((1, 1, STACK, tq), lambda bi, g, i: (bi, i, T_QS // STACK + g, 0)),
            pl.BlockSpec((1, s, STACK), lambda bi, g, i: (bi, 0, NAT_KS // STACK + g)),
            pl.BlockSpec((1, nblk, STACK, tq), lambda bi, g, i: (bi, 0, T_VS // STACK + g, 0)),
        ],
        out_specs=pl.BlockSpec((1, tq, STACK), lambda bi, g, i: (bi, i, g)),
        scratch_shapes=[pltpu.VMEM((STACK, m), BF16), pltpu.VMEM((STACK, tq), F32),
                        pltpu.VMEM((1, m), F32)],
        compiler_params=_cparams(("arbitrary", "arbitrary", "arbitrary")),
        name="sb_attn",
    )(tri, proj_t, nat3, proj_t)


ROUTER_ROWS = 8 + N_EXPERTS


def _post_attn_kernel(x_ref, oa_ref, ob_ref, ga_ref, gb_ref, wbd_ref, wbs_ref, wo_ref, g_ref,
                      wrh_ref, wrl_ref, tri_ref,
                      x1_ref, xn_ref, route_ref, gates_ref, counts_ref, cnt_sc, *, tm, d_model):
    @pl.when(pl.program_id(0) == 0)
    def _():
        cnt_sc[...] = jnp.zeros(cnt_sc.shape, F32)

    ga = ga_ref[...].astype(F32)
    gb = gb_ref[...].astype(F32)
    mixed = ga * _dot(oa_ref[...], wbd_ref[...]) + gb * _dot(ob_ref[...], wbs_ref[...])
    x1 = x_ref[...] + _dot(mixed.astype(BF16), wo_ref[...])
    x1_ref[...] = x1
    ms = jnp.mean(x1 * x1, axis=-1, keepdims=True)
    xn = (x1 * lax.rsqrt(ms + EPS)) * g_ref[...]
    xn_ref[...] = xn

    xh, xl = _split_bf16(xn)
    wrh = wrh_ref[...]
    logits = _nt_dot(wrh, xh) + _nt_dot(wrh, xl) + _nt_dot(wrl_ref[...], xh)

    gl = [logits[r:r + 1] for r in range(N_GROUPS)]
    gmax = functools.reduce(jnp.maximum, gl)
    grp = jnp.full((1, tm), N_GROUPS - 1, I32)
    for r in range(N_GROUPS - 2, -1, -1):
        grp = jnp.where(gl[r] == gmax, r, grp)
    pg = 1.0 / functools.reduce(lambda a, b: a + b, [jnp.exp(v - gmax) for v in gl])

    el = logits[8:8 + EXPERTS_PER_GROUP]
    for r in range(1, N_GROUPS):
        lo = 8 + r * EXPERTS_PER_GROUP
        el = jnp.where(grp == r, logits[lo:lo + EXPERTS_PER_GROUP], el)
    ex = jnp.exp(el - jnp.max(el, axis=0, keepdims=True))
    prob = ex / jnp.sum(ex, axis=0, keepdims=True)
    sub = lax.broadcasted_iota(I32, (EXPERTS_PER_GROUP, tm), 0).astype(F32)
    none = float(EXPERTS_PER_GROUP)
    v1 = jnp.max(prob, axis=0, keepdims=True)
    i1 = jnp.min(jnp.where(prob == v1, sub, none), axis=0, keepdims=True)
    rest = jnp.where(sub == i1, -1.0, prob)
    v2 = jnp.max(rest, axis=0, keepdims=True)
    i2 = jnp.min(jnp.where(rest == v2, sub, none), axis=0, keepdims=True)
    denom = v1 + v2
    gate1 = pg * v1 / denom
    gate2 = pg * v2 / denom
    e1 = grp * EXPERTS_PER_GROUP + i1.astype(I32)
    e2 = grp * EXPERTS_PER_GROUP + i2.astype(I32)

    eio = lax.broadcasted_iota(I32, (N_EXPERTS, tm), 0)
    hit1 = eio == e1
    hit2 = eio == e2
    onehot = jnp.where(hit1, 1.0, 0.0) + jnp.where(hit2, 1.0, 0.0)
    prefix = _dot(onehot.astype(BF16), tri_ref[...]) + cnt_sc[...]
    r1 = jnp.sum(jnp.where(hit1, prefix, 0.0), axis=0, keepdims=True)
    r2 = jnp.sum(jnp.where(hit2, prefix, 0.0), axis=0, keepdims=True)
    cnt = cnt_sc[...] + jnp.sum(onehot, axis=1, keepdims=True)
    cnt_sc[...] = cnt
    counts_ref[...] = jnp.broadcast_to(cnt, counts_ref.shape).astype(I32)

    row = lax.broadcasted_iota(I32, (8, tm), 0)
    route_ref[...] = jnp.where(row == 0, e1, jnp.where(row == 1, e2, jnp.where(
        row == 2, r1.astype(I32), jnp.where(row == 3, r2.astype(I32), 0))))
    gates_ref[...] = jnp.where(row == 0, gate1, jnp.where(row == 1, gate2, 0.0))


def _post_attn(x2, oa, ob, nat, wbd, wbs, wo, ln_g, wrh, wrl, tri, *, tm):
    t, d_model = x2.shape
    kern = functools.partial(_post_attn_kernel, tm=tm, d_model=d_model)
    gate_blk = NAT_GATE // d_model
    assert gate_blk * d_model == NAT_GATE
    const = lambda i: (0, 0)
    return pl.pallas_call(
        kern,
        out_shape=(
            jax.ShapeDtypeStruct((t, d_model), F32),
            jax.ShapeDtypeStruct((t, d_model), F32),
            jax.ShapeDtypeStruct((8, t), I32),
            jax.ShapeDtypeStruct((8, t), F32),
            jax.ShapeDtypeStruct((N_EXPERTS, V7X_LANES), I32),
        ),
        grid=(t // tm,),
        in_specs=[
            pl.BlockSpec((tm, d_model), lambda i: (i, 0)),
            pl.BlockSpec((tm, VA_COLS), lambda i: (i, 0)),
            pl.BlockSpec((tm, SB_COLS), lambda i: (i, 0)),
            pl.BlockSpec((tm, d_model), lambda i: (i, gate_blk)),
            pl.BlockSpec((tm, d_model), lambda i: (i, gate_blk + 1)),
            pl.BlockSpec((VA_COLS, d_model), const),
            pl.BlockSpec((SB_COLS, d_model), const),
            pl.BlockSpec((d_model, d_model), const),
            pl.BlockSpec((1, d_model), const),
            pl.BlockSpec((ROUTER_ROWS, d_model), const),
            pl.BlockSpec((ROUTER_ROWS, d_model), const),
            pl.BlockSpec((tm, tm), const),
        ],
        out_specs=(
            pl.BlockSpec((tm, d_model), lambda i: (i, 0)),
            pl.BlockSpec((tm, d_model), lambda i: (i, 0)),
            pl.BlockSpec((8, tm), lambda i: (0, i)),
            pl.BlockSpec((8, tm), lambda i: (0, i)),
            pl.BlockSpec((N_EXPERTS, V7X_LANES), const),
        ),
        scratch_shapes=[pltpu.VMEM((N_EXPERTS, 1), F32)],
        compiler_params=_cparams(("arbitrary",)),
        name="post_attn",
    )(x2, oa, ob, nat, nat, wbd, wbs, wo, ln_g, wrh, wrl, tri)


ROW_UNROLL = 8


def _row_copy(src, src_row, dst, dst_row, sem):
    return pltpu.make_async_copy(src.at[pl.ds(src_row, 1)], dst.at[pl.ds(dst_row, 1)], sem)


def _dispatch_kernel(lastblk_ref, nblk_ref, dest_ref, xn_ref, xs_ref, zero_sc, sem, zsem,
                     *, tm, tmb, nblk_max):
    @pl.when(pl.program_id(0) == 0)
    def _():
        zero_sc[...] = jnp.zeros(zero_sc.shape, F32)

        def zero_block(row):
            return pltpu.make_async_copy(
                zero_sc, xs_ref.at[pl.ds(pl.multiple_of(row, tmb), tmb)], zsem)

        for go in (lambda c: c.start(), lambda c: c.wait()):
            def seg_block(e, carry, go=go):
                @pl.when(lastblk_ref[e] >= 0)
                def _():
                    go(zero_block(lastblk_ref[e]))
                return carry

            def tail_block(b, carry, go=go):
                go(zero_block(b * tmb))
                return carry

            lax.fori_loop(0, N_EXPERTS, seg_block, 0)
            lax.fori_loop(nblk_ref[0], nblk_max, tail_block, 0)

    def start(r, carry):
        for k in range(2):
            _row_copy(xn_ref, r, xs_ref, dest_ref[k, r], sem).start()
        return carry

    lax.fori_loop(0, tm, start, 0, unroll=ROW_UNROLL)

    def wait(r, carry):
        for k in range(2):
            _row_copy(xn_ref, 0, xs_ref, 0, sem).wait()
        return carry

    lax.fori_loop(0, tm, wait, 0, unroll=ROW_UNROLL)


def _dispatch(lastblk, nblk, dest, xn, *, tmb, nblk_max, tm):
    t, d_model = xn.shape
    kern = functools.partial(_dispatch_kernel, tm=tm, tmb=tmb, nblk_max=nblk_max)
    return pl.pallas_call(
        kern,
        out_shape=jax.ShapeDtypeStruct((nblk_max * tmb, d_model), F32),
        grid_spec=pltpu.PrefetchScalarGridSpec(
            num_scalar_prefetch=2,
            grid=(t // tm,),
            in_specs=[
                pl.BlockSpec((8, tm), lambda i, *_: (0, i), memory_space=pltpu.SMEM),
                pl.BlockSpec((tm, d_model), lambda i, *_: (i, 0)),
            ],
            out_specs=pl.BlockSpec(memory_space=pl.ANY),
            scratch_shapes=[pltpu.VMEM((tmb, d_model), F32), pltpu.SemaphoreType.DMA,
                            pltpu.SemaphoreType.DMA],
        ),
        compiler_params=_cparams(("arbitrary",)),
        name="dispatch",
    )(lastblk, nblk, dest, xn)


def _combine_kernel(dest_ref, dest_next_ref, gates_ref, x1_ref, y_ref, o_ref, buf, sems, *, tm):
    i = pl.program_id(0)
    slot = i % 2

    def issue(idx_ref, to_slot):
        def start(r, carry):
            for k in range(2):
                _row_copy(y_ref, idx_ref[k, r], buf.at[to_slot, k], r, sems.at[to_slot]).start(priority=k)
            return carry

        lax.fori_loop(0, tm, start, 0, unroll=ROW_UNROLL)

    @pl.when(i == 0)
    def _():
        issue(dest_ref, 0)

    @pl.when(i + 1 < pl.num_programs(0))
    def _():
        issue(dest_next_ref, 1 - slot)

    def wait(r, carry):
        for k in range(2):
            _row_copy(y_ref, 0, buf.at[slot, k], 0, sems.at[slot]).wait()
        return carry

    lax.fori_loop(0, tm, wait, 0, unroll=ROW_UNROLL)

    gt = jnp.transpose(gates_ref[...])
    o_ref[...] = x1_ref[...] + gt[:, 0:1] * buf[slot, 0] + gt[:, 1:2] * buf[slot, 1]


def _combine(dest, gates, x1, y, *, tm):
    t, d_model = x1.shape
    n = t // tm
    kern = functools.partial(_combine_kernel, tm=tm)
    return pl.pallas_call(
        kern,
        out_shape=jax.ShapeDtypeStruct((t, d_model), F32),
        grid=(n,),
        in_specs=[
            pl.BlockSpec((8, tm), lambda i: (0, i), memory_space=pltpu.SMEM),
            pl.BlockSpec((8, tm), lambda i: (0, jnp.minimum(i + 1, n - 1)), memory_space=pltpu.SMEM),
            pl.BlockSpec((8, tm), lambda i: (0, i)),
            pl.BlockSpec((tm, d_model), lambda i: (i, 0)),
            pl.BlockSpec(memory_space=pl.ANY),
        ],
        out_specs=pl.BlockSpec((tm, d_model), lambda i: (i, 0)),
        scratch_shapes=[pltpu.VMEM((2, 2, tm, d_model), F32), pltpu.SemaphoreType.DMA((2,))],
        compiler_params=_cparams(("arbitrary",)),
        name="combine",
    )(dest, dest, gates, x1, y)


def _experts_kernel(blk_e_ref, nblk_ref, xs_ref, wg_ref, wu_ref, wd_ref, y_ref,
                    wgu_sc, wd_sc, *, d_ff):
    b = pl.program_id(0)
    active = b < nblk_ref[0]

    @pl.when(active)
    def _():
        prev = blk_e_ref[jnp.maximum(b - 1, 0)]

        @pl.when(jnp.logical_or(b == 0, blk_e_ref[b] != prev))
        def _():
            wgu_sc[:, :d_ff] = wg_ref[0, 0].astype(BF16)
            wgu_sc[:, d_ff:] = wu_ref[0, 0].astype(BF16)
            wd_sc[...] = wd_ref[0, 0].astype(BF16)

        gu = _dot(xs_ref[...].astype(BF16), wgu_sc[...])
        gate = gu[:, :d_ff]
        act = (gate * jax.nn.sigmoid(gate)) * gu[:, d_ff:]
        y_ref[...] = _dot(act.astype(BF16), wd_sc[...])

    @pl.when(jnp.logical_not(active))
    def _():
        y_ref[...] = jnp.zeros(y_ref.shape, F32)


def _experts(blk_e, nblk, xs, w_gate, w_up, w_down, *, layer, tmb):
    rows, d_model = xs.shape
    d_ff = w_gate.shape[-1]
    kern = functools.partial(_experts_kernel, d_ff=d_ff)

    def x_map(b, be, nb):
        return (jnp.minimum(b, nb[0] - 1), 0)

    def w_map(b, be, nb):
        return (layer, be[b], 0, 0)

    return pl.pallas_call(
        kern,
        out_shape=jax.ShapeDtypeStruct((rows, d_model), F32),
        grid_spec=pltpu.PrefetchScalarGridSpec(
            num_scalar_prefetch=2,
            grid=(rows // tmb,),
            in_specs=[
                pl.BlockSpec((tmb, d_model), x_map),
                pl.BlockSpec((1, 1, d_model, d_ff), w_map),
                pl.BlockSpec((1, 1, d_model, d_ff), w_map),
                pl.BlockSpec((1, 1, d_ff, d_model), w_map),
            ],
            out_specs=pl.BlockSpec((tmb, d_model), lambda b, be, nb: (b, 0)),
            scratch_shapes=[pltpu.VMEM((d_model, 2 * d_ff), BF16), pltpu.VMEM((d_ff, d_model), BF16)],
        ),
        compiler_params=_cparams(("arbitrary",)),
        name="experts",
    )(blk_e, nblk, xs, w_gate, w_up, w_down)


class _Tiles:
    def __init__(self, t, s):
        self.attn = min(256, s)
        self.proj = min(512, s)
        self.rows = min(512, t)
        self.expert = min(512, t)


def _strict_upper(n):
    a = jnp.arange(n)
    return (a[:, None] < a[None, :]).astype(BF16)


def _block_table(counts, tmb, nblk_max):
    padded = ((counts + tmb - 1) // tmb) * tmb
    pend = jnp.cumsum(padded)
    pstart = (pend - padded).astype(I32)
    blk_first = jnp.arange(nblk_max, dtype=I32) * tmb
    blk_e = jnp.minimum(jnp.sum(pend[None, :] <= blk_first[:, None], axis=1), N_EXPERTS - 1).astype(I32)
    nblk = (pend[-1] // tmb).astype(I32)
    last_e = jnp.sum(jnp.where(jnp.arange(nblk_max) == nblk - 1, blk_e, 0))
    blk_e = jnp.where(jnp.arange(nblk_max) < nblk, blk_e, last_e)
    lastblk = jnp.where(padded > 0, pend - tmb, -1).astype(I32)
    return pstart, lastblk, blk_e, nblk.reshape(1)


def kernel(x, rel_bias, ln1_g, w_in, qnorm_g, knorm_g, lambda_q1, lambda_k1, lambda_q2, lambda_k2,
           subln_g, w_branch_diff, w_branch_sb, w_out, ln2_g, w_group, w_router, w_gate, w_up, w_down):
    b, s, d_model = x.shape
    depth = w_in.shape[0]
    t = b * s
    tiles = _Tiles(t, s)
    nblk_max = 2 * t // tiles.expert + N_EXPERTS

    bias = _bias_tiles_t(rel_bias, tiles.attn) * LOG2_E
    head = jnp.arange(STACK) // HEAD_DIM
    seg = jnp.where(head[:, None] == head[None, :], 1.0 / HEAD_DIM, 0.0).astype(BF16)
    tri_attn = _strict_upper(tiles.attn)
    tri_tok = _strict_upper(tiles.proj)

    c_k, c_va = 2 * STACK, QK_COLS
    c_qs = c_va + VA_COLS
    c_ks, c_vs, c_g = c_qs + SB_COLS, c_qs + 2 * SB_COLS, c_qs + 3 * SB_COLS

    x2 = x.reshape(t, d_model)
    for l in range(depth):
        lam_init = 0.8 - 0.6 * math.exp(-0.3 * l)
        w = w_in[l].astype(BF16)
        wn = jnp.concatenate([w[:, c_k:c_va], w[:, c_ks:c_vs], w[:, c_g:]], axis=1)
        wt = jnp.transpose(jnp.concatenate(
            [w[:, :c_k], w[:, c_va:c_qs], w[:, c_qs:c_ks], w[:, c_vs:c_g]], axis=1))
        n_heads = 2 * N_DIFF_HEADS
        k_gain = jnp.tile(knorm_g[l].astype(F32), n_heads).reshape(1, 2 * STACK)
        q_gain = jnp.tile(qnorm_g[l].astype(F32) * (HEAD_DIM ** -0.5 * LOG2_E), n_heads)
        q_gain = jnp.broadcast_to(q_gain[:, None], (2 * STACK, tiles.proj))
        lamv = jnp.stack([lambda_q1[l], lambda_k1[l], lambda_q2[l], lambda_k2[l]]).astype(F32)

        nat, proj_t = _in_proj(x2, ln1_g[l].reshape(1, d_model), wn, wt, k_gain, seg, q_gain,
                               batch=b, seq=s, tm=tiles.proj, blk=tiles.attn)
        nat3 = nat.reshape(b, s, nat.shape[-1])
        oa = _diff_attn(nat3, proj_t, lamv, subln_g[l].reshape(1, DIFF_V_DIM), bias,
                        tq=tiles.attn, lam_init=lam_init)
        ob = _sb_attn(nat3, proj_t, tri_attn, tq=tiles.attn)

        w_rt = jnp.concatenate([w_group[l].T, jnp.zeros((8 - N_GROUPS, d_model), F32), w_router[l].T])
        wrh, wrl = _split_bf16(w_rt.astype(F32))
        x1, xn, route, gates, counts = _post_attn(
            x2, oa.reshape(t, VA_COLS), ob.reshape(t, SB_COLS), nat,
            w_branch_diff[l].astype(BF16), w_branch_sb[l].astype(BF16), w_out[l].astype(BF16),
            ln2_g[l].reshape(1, d_model), wrh, wrl, tri_tok, tm=tiles.proj)

        pstart, lastblk, blk_e, nblk = _block_table(counts[:, 0], tiles.expert, nblk_max)
        seg_start = jnp.sum(jnp.where(route[:2, :, None] == jnp.arange(N_EXPERTS), pstart, 0), axis=-1)
        dest = jnp.concatenate([seg_start + route[2:4], jnp.zeros((6, t), I32)])
        xs = _dispatch(lastblk, nblk, dest, xn, tmb=tiles.expert, nblk_max=nblk_max, tm=tiles.rows)
        y = _experts(blk_e, nblk, xs, w_gate, w_up, w_down, layer=l, tmb=tiles.expert)
        x2 = _combine(dest, gates, x1, y, tm=tiles.rows)
    return x2.reshape(b, s, d_model)
```

```python
import functools
import math

import jax
import jax.numpy as jnp
from jax import lax
from jax.experimental import pallas as pl
from jax.experimental.pallas import tpu as pltpu

F32 = jnp.float32
BF16 = jnp.bfloat16
I32 = jnp.int32

HEAD_DIM = 64
N_DIFF_HEADS = 4
DIFF_V_DIM = 2 * HEAD_DIM
N_SB_HEADS = 8
N_BUCKETS = 32
MAX_DISTANCE = 128
N_GROUPS = 4
EXPERTS_PER_GROUP = 8
N_EXPERTS = N_GROUPS * EXPERTS_PER_GROUP
EPS = 1e-6
NEG_INF = -1e30
LOG2_E = math.log2(math.e)

V7X_LANES = 128
VMEM_LIMIT = 56 * 1024 * 1024

F32_EXP_ZERO_BELOW = -104.0

HEADS_PER_STACK = 4
STACK = HEADS_PER_STACK * HEAD_DIM

QK_COLS = 4 * N_DIFF_HEADS * HEAD_DIM
VA_COLS = N_DIFF_HEADS * DIFF_V_DIM
SB_COLS = N_SB_HEADS * HEAD_DIM
NAT_K, NAT_KS, NAT_GATE = 0, 2 * STACK, 4 * STACK
T_Q, T_VA, T_QS, T_VS = 0, 2 * STACK, 4 * STACK, 6 * STACK
T_ROWS = 8 * STACK


def _cparams(semantics, vmem=VMEM_LIMIT):
    return pltpu.CompilerParams(dimension_semantics=semantics, vmem_limit_bytes=vmem)


def _nt_dot(a, b):
    return lax.dot_general(a, b, (((1,), (1,)), ((), ())), preferred_element_type=F32)


def _dot(a, b):
    return jnp.dot(a, b, preferred_element_type=F32)


def _split_bf16(x):
    hi = x.astype(BF16)
    lo = (x - hi.astype(F32)).astype(BF16)
    return hi, lo


def _in_proj_kernel(x_ref, g_ref, wn_ref, wt_ref, kg_ref, seg_ref, qg_ref, on_ref, ot_ref,
                    *, tm, d_model, blk):
    x = x_ref[...]
    ms = jnp.mean(x * x, axis=-1, keepdims=True)
    h = ((x * lax.rsqrt(ms + EPS)) * g_ref[...]).astype(BF16)

    seg = seg_ref[...]
    for col in range(NAT_K, NAT_KS, STACK):
        acc = _dot(h, wn_ref[:, col:col + STACK])
        hi, lo = _split_bf16(acc * acc)
        msq = _dot(hi, seg) + _dot(lo, seg)
        acc = (acc * lax.rsqrt(msq + EPS)) * kg_ref[:, col:col + STACK]
        on_ref[:, col:col + STACK] = acc.astype(BF16)
    width = 2 * STACK
    on_ref[:, NAT_KS:NAT_GATE] = _dot(h, wn_ref[:, NAT_KS:NAT_GATE]).astype(BF16)
    for col in range(NAT_GATE, NAT_GATE + 2 * d_model, width):
        on_ref[:, col:col + width] = jax.nn.sigmoid(_dot(h, wn_ref[:, col:col + width])).astype(BF16)

    for row in range(0, T_ROWS, width):
        acc = _nt_dot(wt_ref[row:row + width, :], h)
        if row == T_Q:
            a3 = acc.reshape(width // HEAD_DIM, HEAD_DIM, tm)
            msq = jnp.mean(a3 * a3, axis=1, keepdims=True)
            acc = (a3 * lax.rsqrt(msq + EPS)).reshape(width, tm)
            acc = acc * qg_ref[...]
        elif row == T_QS:
            acc = acc * (HEAD_DIM ** -0.5)
        for c in range(tm // blk):
            ot_ref[0, c, row:row + width, :] = acc[:, c * blk:(c + 1) * blk].astype(BF16)


def _in_proj(x2, ln_g, wn, wt, k_gain, seg, q_gain, *, batch, seq, tm, blk):
    t, d_model = x2.shape
    nat_cols = wn.shape[1]
    tiles_per_batch = seq // tm
    kern = functools.partial(_in_proj_kernel, tm=tm, d_model=d_model, blk=blk)
    const = lambda i: (0, 0)
    return pl.pallas_call(
        kern,
        out_shape=(jax.ShapeDtypeStruct((t, nat_cols), BF16),
                   jax.ShapeDtypeStruct((batch, seq // blk, T_ROWS, blk), BF16)),
        grid=(t // tm,),
        in_specs=[
            pl.BlockSpec((tm, d_model), lambda i: (i, 0)),
            pl.BlockSpec((1, d_model), const),
            pl.BlockSpec((d_model, nat_cols), const),
            pl.BlockSpec((T_ROWS, d_model), const),
            pl.BlockSpec((1, 2 * STACK), const),
            pl.BlockSpec((STACK, STACK), const),
            pl.BlockSpec((2 * STACK, tm), const),
        ],
        out_specs=(
            pl.BlockSpec((tm, nat_cols), lambda i: (i, 0)),
            pl.BlockSpec((1, tm // blk, T_ROWS, blk),
                         lambda i: (i // tiles_per_batch, i % tiles_per_batch, 0, 0)),
        ),
        compiler_params=_cparams(("arbitrary",)),
        name="in_proj",
    )(x2, ln_g, wn, wt, k_gain, seg, q_gain)


def _stack_heads_t(q_t, qs_sc, tq):
    qf = q_t.astype(F32)
    row_head = lax.broadcasted_iota(I32, (STACK, tq), 0) // HEAD_DIM
    for h in range(HEADS_PER_STACK):
        qs_sc[:, h * tq:(h + 1) * tq] = jnp.where(row_head == h, qf, 0.0).astype(BF16)


def _t5_bucket(n):
    max_exact = N_BUCKETS // 2
    nf = jnp.maximum(n, max_exact).astype(F32)
    large = max_exact + (jnp.log(nf / max_exact) / math.log(MAX_DISTANCE / max_exact)
                         * (N_BUCKETS - max_exact)).astype(I32)
    large = jnp.minimum(large, N_BUCKETS - 1)
    return jnp.where(n < max_exact, n, large)


def _bias_tiles_t(rel_bias, tq):
    assert tq + 1 >= MAX_DISTANCE, "far blocks must lie entirely in the last bucket"
    dist = jnp.arange(tq, dtype=I32)[None, :] - jnp.arange(tq, dtype=I32)[:, None]
    rb = rel_bias.astype(F32)

    def tile(d):
        onehot = (_t5_bucket(jnp.maximum(d, 0))[..., None] == jnp.arange(N_BUCKETS)).astype(F32)
        return jnp.einsum("kqb,bh->khq", onehot, rb, precision=lax.Precision.HIGHEST)

    far = tile(jnp.full((1, 1), 2 * tq, I32))
    near = tile(dist + tq) - far
    diag = jnp.where(dist[:, None, :] >= 0, tile(dist) - far, NEG_INF)
    return jnp.stack([near, diag]).reshape(2, tq, N_DIFF_HEADS * tq)


ONES_ROWS = 16
FAR_BLOCKS = 2
BF16_NORM_MARGIN = 1.02
SAFE_EXP2_SPAN = 100.0


def _diff_attn_kernel(bound_ref, lam_ref, subg_ref, bias_ref, q_ref, k_ref, v_ref, o_ref,
                      qs_sc, m_sc, acc_sc, *, tq, lam_init):
    i = pl.program_id(1)
    for mp in range(2):
        _stack_heads_t(q_ref[0, 0, mp * STACK:(mp + 1) * STACK, :], qs_sc.at[mp], tq)
    m_sc[...] = jnp.full(m_sc.shape, NEG_INF, F32)
    acc_sc[...] = jnp.zeros(acc_sc.shape, F32)
    ones = jnp.ones((ONES_ROWS, tq), BF16)

    def step(j, nb, bias_idx, fixed_max):
        start = pl.multiple_of(j * tq, tq)
        chains = [(mp, h) for mp in range(2) for h in range(N_DIFF_HEADS)]
        scores = []
        for mp, h in chains:
            kb = k_ref[0, pl.ds(start, nb * tq), mp * STACK:(mp + 1) * STACK]
            scores.append(_dot(kb, qs_sc[mp, :, h * tq:(h + 1) * tq]))
        probs = []
        for (mp, h), s in zip(chains, scores):
            cols = slice(h * tq, (h + 1) * tq)
            if bias_idx == "diagonal":
                s = s + bias_ref[1, :, cols]
            elif bias_idx == "near+diagonal":
                s = s + bias_ref[:, :, cols].reshape(2 * tq, tq)
            if fixed_max is None:
                m_old = m_sc[mp, :, cols]
                m_new = jnp.maximum(m_old, jnp.max(s, axis=0, keepdims=True))
                m_sc[mp, :, cols] = m_new
                probs.append((jnp.exp2(m_old - m_new), jnp.exp2(s - m_new).astype(BF16)))
            else:
                probs.append((None, jnp.exp2(s - fixed_max).astype(BF16)))
        for (mp, h), (alpha, pb) in zip(chains, probs):
            rows = slice(h * DIFF_V_DIM, (h + 1) * DIFF_V_DIM)
            pv = None
            for blk in range(nb):
                v_ext = jnp.concatenate([v_ref[0, j + blk, rows, :], ones], axis=0)
                part = _dot(v_ext, pb[blk * tq:(blk + 1) * tq])
                pv = part if pv is None else pv + part
            if alpha is None:
                acc_sc[mp, h] += pv
            else:
                acc_sc[mp, h] = alpha * acc_sc[mp, h] + pv

    def sweep(fixed_max):
        n_far = jnp.maximum(i - 1, 0)

        def far_body(jj, carry):
            step(FAR_BLOCKS * jj, FAR_BLOCKS, None, fixed_max)
            return carry

        lax.fori_loop(0, n_far // FAR_BLOCKS, far_body, 0)
        for rem in range(1, FAR_BLOCKS):
            @pl.when(n_far % FAR_BLOCKS == rem)
            def _(rem=rem):
                step(n_far - rem, rem, None, fixed_max)

        @pl.when(i == 0)
        def _():
            step(i, 1, "diagonal", fixed_max)

        @pl.when(i > 0)
        def _():
            step(i - 1, 2, "near+diagonal", fixed_max)

    safe_bound = bound_ref[1] > 0.5

    @pl.when(safe_bound)
    def _():
        sweep(bound_ref[0])

    @pl.when(jnp.logical_not(safe_bound))
    def _():
        sweep(None)

    lamv = lam_ref[...]
    lam = (jnp.exp(jnp.sum(lamv[0:1] * lamv[1:2], axis=1, keepdims=True))
           - jnp.exp(jnp.sum(lamv[2:3] * lamv[3:4], axis=1, keepdims=True)) + lam_init)
    for h in range(N_DIFF_HEADS):
        rows = slice(h * DIFF_V_DIM, (h + 1) * DIFF_V_DIM)
        a1 = acc_sc[0, h]
        a2 = acc_sc[1, h]
        inv_l1 = 1.0 / a1[DIFF_V_DIM:DIFF_V_DIM + 1]
        inv_l2 = lam / a2[DIFF_V_DIM:DIFF_V_DIM + 1]
        o_t = a1[:DIFF_V_DIM] * inv_l1 - a2[:DIFF_V_DIM] * inv_l2
        o = jnp.transpose(o_t)
        ms = jnp.mean(o * o, axis=-1, keepdims=True)
        o = ((o * lax.rsqrt(ms + EPS)) * subg_ref[...]) * (1.0 - lam_init)
        o_ref[0, :, rows] = o.astype(BF16)


def _diff_attn(nat3, proj_t, bound, lamv, subln_g, bias, *, tq, lam_init):
    b, s, _ = nat3.shape
    nblk = s // tq
    m = N_DIFF_HEADS * tq
    kern = functools.partial(_diff_attn_kernel, tq=tq, lam_init=lam_init)
    two = 2 * STACK
    return pl.pallas_call(
        kern,
        out_shape=jax.ShapeDtypeStruct((b, s, VA_COLS), BF16),
        grid=(b, nblk),
        in_specs=[
            pl.BlockSpec(memory_space=pltpu.SMEM),
            pl.BlockSpec((4, HEAD_DIM), lambda bi, i: (0, 0)),
            pl.BlockSpec((1, DIFF_V_DIM), lambda bi, i: (0, 0)),
            pl.BlockSpec((2, tq, m), lambda bi, i: (0, 0, 0)),
            pl.BlockSpec((1, 1, two, tq), lambda bi, i: (bi, i, T_Q // two, 0)),
            pl.BlockSpec((1, s, two), lambda bi, i: (bi, 0, NAT_K // two)),
            pl.BlockSpec((1, nblk, two, tq), lambda bi, i: (bi, 0, T_VA // two, 0)),
        ],
        out_specs=pl.BlockSpec((1, tq, VA_COLS), lambda bi, i: (bi, i, 0)),
        scratch_shapes=[
            pltpu.VMEM((2, STACK, m), BF16),
            pltpu.VMEM((2, 1, m), F32),
            pltpu.VMEM((2, N_DIFF_HEADS, DIFF_V_DIM + ONES_ROWS, tq), F32),
        ],
        compiler_params=_cparams(("arbitrary", "arbitrary")),
        name="diff_attn",
    )(bound, lamv, subln_g, bias, proj_t, nat3, proj_t)


def _score_bound(q_gain, k_gain, bias):
    qk = HEAD_DIM * jnp.max(jnp.abs(q_gain)) * jnp.max(jnp.abs(k_gain)) * BF16_NORM_MARGIN
    finite = bias > 0.5 * NEG_INF
    b_hi = jnp.maximum(jnp.max(jnp.where(finite, bias, 0.0)), 0.0)
    b_lo = jnp.minimum(jnp.min(jnp.where(finite, bias, 0.0)), 0.0)
    bound = qk + b_hi
    safe = (2.0 * qk + (b_hi - b_lo)) <= SAFE_EXP2_SPAN
    return jnp.stack([bound, safe.astype(F32)]).astype(F32)


def _sb_attn_kernel(tri_ref, q_ref, k_ref, v_ref, o_ref, qs_sc, acc_sc, c_sc, *, tq):
    i = pl.program_id(2)
    m = HEADS_PER_STACK * tq
    _stack_heads_t(q_ref[0, 0], qs_sc, tq)
    acc_sc[...] = jnp.zeros(acc_sc.shape, F32)
    c_sc[...] = jnp.zeros(c_sc.shape, F32)

    def step(blocks):
        mask = (lax.broadcasted_iota(I32, (tq, tq), 0)
                < lax.broadcasted_iota(I32, (tq, tq), 1))
        chains = [(j, diag, h) for j, diag in blocks for h in range(HEADS_PER_STACK)]
        tri = tri_ref[...]
        zs = []
        for j, _, h in chains:
            kb = k_ref[0, pl.ds(pl.multiple_of(j * tq, tq), tq), :]
            zs.append(_dot(kb, qs_sc[:, h * tq:(h + 1) * tq]))
        parts = []
        for (j, diag, h), z in zip(chains, zs):
            nz = -z
            log_1m = jnp.minimum(nz, 0.0) - jnp.log(1.0 + jnp.exp(jnp.minimum(z, nz)))
            log_sig = z + log_1m
            if diag:
                log_1m = jnp.where(mask, log_1m, 0.0)
            parts.append((log_sig, log_1m) + _split_bf16(log_1m))
        sufs = [_dot(tri, hi) + _dot(tri, lo) for _, _, hi, lo in parts]
        c = [c_sc[:, h * tq:(h + 1) * tq] for h in range(HEADS_PER_STACK)]
        weights = []
        for (j, diag, h), (log_sig, log_1m, _, _), suf in zip(chains, parts, sufs):
            a = jnp.exp(log_sig + (suf + c[h]))
            if diag:
                a = jnp.where(mask, a, 0.0)
            c[h] = c[h] + jnp.sum(log_1m, axis=0, keepdims=True)
            weights.append(a.astype(BF16))
        for h in range(HEADS_PER_STACK):
            c_sc[:, h * tq:(h + 1) * tq] = c[h]
        for (j, _, h), a in zip(chains, weights):
            rows = slice(h * HEAD_DIM, (h + 1) * HEAD_DIM)
            acc_sc[rows, :] += _dot(v_ref[0, j, rows, :], a)

    @pl.when(i == 0)
    def _():
        step([(i, True)])

    @pl.when(i > 0)
    def _():
        step([(i, True), (i - 1, False)])

    def cond(j):
        return jnp.logical_and(j >= 0, jnp.max(c_sc[...]) > F32_EXP_ZERO_BELOW)

    def body(j):
        step([(j, False)])
        return j - 1

    lax.while_loop(cond, body, i - 2)

    o_ref[0] = jnp.transpose(acc_sc[...]).astype(BF16)


def _sb_attn(nat3, proj_t, tri, *, tq):
    b, s, _ = nat3.shape
    nblk = s // tq
    n_stacks = SB_COLS // STACK
    m = HEADS_PER_STACK * tq
    kern = functools.partial(_sb_attn_kernel, tq=tq)
    return pl.pallas_call(
        kern,
        out_shape=jax.ShapeDtypeStruct((b, s, SB_COLS), BF16),
        grid=(b, n_stacks, nblk),
        in_specs=[
            pl.BlockSpec((tq, tq), lambda bi, g, i: (0, 0)),
            pl.BlockSpec((1, 1, STACK, tq), lambda bi, g, i: (bi, i, T_QS // STACK + g, 0)),
            pl.BlockSpec((1, s, STACK), lambda bi, g, i: (bi, 0, NAT_KS // STACK + g)),
            pl.BlockSpec((1, nblk, STACK, tq), lambda bi, g, i: (bi, 0, T_VS // STACK + g, 0)),
        ],
        out_specs=pl.BlockSpec((1, tq, STACK), lambda bi, g, i: (bi, i, g)),
        scratch_shapes=[pltpu.VMEM((STACK, m), BF16), pltpu.VMEM((STACK, tq), F32),
                        pltpu.VMEM((1, m), F32)],
        compiler_params=_cparams(("arbitrary", "arbitrary", "arbitrary")),
        name="sb_attn",
    )(tri, proj_t, nat3, proj_t)


ROUTER_ROWS = 8 + N_EXPERTS


def _post_attn_kernel(x_ref, oa_ref, ob_ref, ga_ref, gb_ref, wbd_ref, wbs_ref, wo_ref, g_ref,
                      wrh_ref, wrl_ref, tri_ref,
                      x1_ref, xn_ref, route_ref, gates_ref, counts_ref, cnt_sc, *, tm, d_model):
    @pl.when(pl.program_id(0) == 0)
    def _():
        cnt_sc[...] = jnp.zeros(cnt_sc.shape, F32)

    ga = ga_ref[...].astype(F32)
    gb = gb_ref[...].astype(F32)
    mixed = ga * _dot(oa_ref[...], wbd_ref[...]) + gb * _dot(ob_ref[...], wbs_ref[...])
    x1 = x_ref[...] + _dot(mixed.astype(BF16), wo_ref[...])
    x1_ref[...] = x1
    ms = jnp.mean(x1 * x1, axis=-1, keepdims=True)
    xn = (x1 * lax.rsqrt(ms + EPS)) * g_ref[...]
    xn_ref[...] = xn

    xh, xl = _split_bf16(xn)
    wrh = wrh_ref[...]
    logits = _nt_dot(wrh, xh) + _nt_dot(wrh, xl) + _nt_dot(wrl_ref[...], xh)

    gl = [logits[r:r + 1] for r in range(N_GROUPS)]
    gmax = functools.reduce(jnp.maximum, gl)
    grp = jnp.full((1, tm), N_GROUPS - 1, I32)
    for r in range(N_GROUPS - 2, -1, -1):
        grp = jnp.where(gl[r] == gmax, r, grp)
    pg = 1.0 / functools.reduce(lambda a, b: a + b, [jnp.exp(v - gmax) for v in gl])

    el = logits[8:8 + EXPERTS_PER_GROUP]
    for r in range(1, N_GROUPS):
        lo = 8 + r * EXPERTS_PER_GROUP
        el = jnp.where(grp == r, logits[lo:lo + EXPERTS_PER_GROUP], el)
    ex = jnp.exp(el - jnp.max(el, axis=0, keepdims=True))
    prob = ex / jnp.sum(ex, axis=0, keepdims=True)
    sub = lax.broadcasted_iota(I32, (EXPERTS_PER_GROUP, tm), 0).astype(F32)
    none = float(EXPERTS_PER_GROUP)
    v1 = jnp.max(prob, axis=0, keepdims=True)
    i1 = jnp.min(jnp.where(prob == v1, sub, none), axis=0, keepdims=True)
    rest = jnp.where(sub == i1, -1.0, prob)
    v2 = jnp.max(rest, axis=0, keepdims=True)
    i2 = jnp.min(jnp.where(rest == v2, sub, none), axis=0, keepdims=True)
    denom = v1 + v2
    gate1 = pg * v1 / denom
    gate2 = pg * v2 / denom
    e1 = grp * EXPERTS_PER_GROUP + i1.astype(I32)
    e2 = grp * EXPERTS_PER_GROUP + i2.astype(I32)

    eio = lax.broadcasted_iota(I32, (N_EXPERTS, tm), 0)
    hit1 = eio == e1
    hit2 = eio == e2
    onehot = jnp.where(hit1, 1.0, 0.0) + jnp.where(hit2, 1.0, 0.0)
    prefix = _dot(onehot.astype(BF16), tri_ref[...]) + cnt_sc[...]
    r1 = jnp.sum(jnp.where(hit1, prefix, 0.0), axis=0, keepdims=True)
    r2 = jnp.sum(jnp.where(hit2, prefix, 0.0), axis=0, keepdims=True)
    cnt = cnt_sc[...] + jnp.sum(onehot, axis=1, keepdims=True)
    cnt_sc[...] = cnt
    counts_ref[...] = jnp.broadcast_to(cnt, counts_ref.shape).astype(I32)

    row = lax.broadcasted_iota(I32, (8, tm), 0)
    route_ref[...] = jnp.where(row == 0, e1, jnp.where(row == 1, e2, jnp.where(
        row == 2, r1.astype(I32), jnp.where(row == 3, r2.astype(I32), 0))))
    gates_ref[...] = jnp.where(row == 0, gate1, jnp.where(row == 1, gate2, 0.0))


def _post_attn(x2, oa, ob, nat, wbd, wbs, wo, ln_g, wrh, wrl, tri, *, tm):
    t, d_model = x2.shape
    kern = functools.partial(_post_attn_kernel, tm=tm, d_model=d_model)
    gate_blk = NAT_GATE // d_model
    assert gate_blk * d_model == NAT_GATE
    const = lambda i: (0, 0)
    return pl.pallas_call(
        kern,
        out_shape=(
            jax.ShapeDtypeStruct((t, d_model), F32),
            jax.ShapeDtypeStruct((t, d_model), F32),
            jax.ShapeDtypeStruct((8, t), I32),
            jax.ShapeDtypeStruct((8, t), F32),
            jax.ShapeDtypeStruct((N_EXPERTS, V7X_LANES), I32),
        ),
        grid=(t // tm,),
        in_specs=[
            pl.BlockSpec((tm, d_model), lambda i: (i, 0)),
            pl.BlockSpec((tm, VA_COLS), lambda i: (i, 0)),
            pl.BlockSpec((tm, SB_COLS), lambda i: (i, 0)),
            pl.BlockSpec((tm, d_model), lambda i: (i, gate_blk)),
            pl.BlockSpec((tm, d_model), lambda i: (i, gate_blk + 1)),
            pl.BlockSpec((VA_COLS, d_model), const),
            pl.BlockSpec((SB_COLS, d_model), const),
            pl.BlockSpec((d_model, d_model), const),
            pl.BlockSpec((1, d_model), const),
            pl.BlockSpec((ROUTER_ROWS, d_model), const),
            pl.BlockSpec((ROUTER_ROWS, d_model), const),
            pl.BlockSpec((tm, tm), const),
        ],
        out_specs=(
            pl.BlockSpec((tm, d_model), lambda i: (i, 0)),
            pl.BlockSpec((tm, d_model), lambda i: (i, 0)),
            pl.BlockSpec((8, tm), lambda i: (0, i)),
            pl.BlockSpec((8, tm), lambda i: (0, i)),
            pl.BlockSpec((N_EXPERTS, V7X_LANES), const),
        ),
        scratch_shapes=[pltpu.VMEM((N_EXPERTS, 1), F32)],
        compiler_params=_cparams(("arbitrary",)),
        name="post_attn",
    )(x2, oa, ob, nat, nat, wbd, wbs, wo, ln_g, wrh, wrl, tri)


ROW_UNROLL = 8


def _row_copy(src, src_row, dst, dst_row, sem):
    return pltpu.make_async_copy(src.at[pl.ds(src_row, 1)], dst.at[pl.ds(dst_row, 1)], sem)


def _dispatch_kernel(lastblk_ref, nblk_ref, dest_ref, xn_ref, xs_ref, zero_sc, sem, zsem,
                     *, tm, tmb, nblk_max):
    @pl.when(pl.program_id(0) == 0)
    def _():
        zero_sc[...] = jnp.zeros(zero_sc.shape, F32)

        def zero_block(row):
            return pltpu.make_async_copy(
                zero_sc, xs_ref.at[pl.ds(pl.multiple_of(row, tmb), tmb)], zsem)

        for go in (lambda c: c.start(), lambda c: c.wait()):
            def seg_block(e, carry, go=go):
                @pl.when(lastblk_ref[e] >= 0)
                def _():
                    go(zero_block(lastblk_ref[e]))
                return carry

            def tail_block(b, carry, go=go):
                go(zero_block(b * tmb))
                return carry

            lax.fori_loop(0, N_EXPERTS, seg_block, 0)
            lax.fori_loop(nblk_ref[0], nblk_max, tail_block, 0)

    def start(r, carry):
        for k in range(2):
            _row_copy(xn_ref, r, xs_ref, dest_ref[k, r], sem).start()
        return carry

    lax.fori_loop(0, tm, start, 0, unroll=ROW_UNROLL)

    def wait(r, carry):
        for k in range(2):
            _row_copy(xn_ref, 0, xs_ref, 0, sem).wait()
        return carry

    lax.fori_loop(0, tm, wait, 0, unroll=ROW_UNROLL)


def _dispatch(lastblk, nblk, dest, xn, *, tmb, nblk_max, tm):
    t, d_model = xn.shape
    kern = functools.partial(_dispatch_kernel, tm=tm, tmb=tmb, nblk_max=nblk_max)
    return pl.pallas_call(
        kern,
        out_shape=jax.ShapeDtypeStruct((nblk_max * tmb, d_model), F32),
        grid_spec=pltpu.PrefetchScalarGridSpec(
            num_scalar_prefetch=2,
            grid=(t // tm,),
            in_specs=[
                pl.BlockSpec((8, tm), lambda i, *_: (0, i), memory_space=pltpu.SMEM),
                pl.BlockSpec((tm, d_model), lambda i, *_: (i, 0)),
            ],
            out_specs=pl.BlockSpec(memory_space=pl.ANY),
            scratch_shapes=[pltpu.VMEM((tmb, d_model), F32), pltpu.SemaphoreType.DMA,
                            pltpu.SemaphoreType.DMA],
        ),
        compiler_params=_cparams(("arbitrary",)),
        name="dispatch",
    )(lastblk, nblk, dest, xn)


def _combine_kernel(dest_ref, dest_next_ref, gates_ref, x1_ref, y_ref, o_ref, buf, sems, *, tm):
    i = pl.program_id(0)
    slot = i % 2

    def issue(idx_ref, to_slot):
        def start(r, carry):
            for k in range(2):
                _row_copy(y_ref, idx_ref[k, r], buf.at[to_slot, k], r, sems.at[to_slot]).start(priority=k)
            return carry

        lax.fori_loop(0, tm, start, 0, unroll=ROW_UNROLL)

    @pl.when(i == 0)
    def _():
        issue(dest_ref, 0)

    @pl.when(i + 1 < pl.num_programs(0))
    def _():
        issue(dest_next_ref, 1 - slot)

    def wait(r, carry):
        for k in range(2):
            _row_copy(y_ref, 0, buf.at[slot, k], 0, sems.at[slot]).wait()
        return carry

    lax.fori_loop(0, tm, wait, 0, unroll=ROW_UNROLL)

    gt = jnp.transpose(gates_ref[...])
    o_ref[...] = x1_ref[...] + gt[:, 0:1] * buf[slot, 0] + gt[:, 1:2] * buf[slot, 1]


def _combine(dest, gates, x1, y, *, tm):
    t, d_model = x1.shape
    n = t // tm
    kern = functools.partial(_combine_kernel, tm=tm)
    return pl.pallas_call(
        kern,
        out_shape=jax.ShapeDtypeStruct((t, d_model), F32),
        grid=(n,),
        in_specs=[
            pl.BlockSpec((8, tm), lambda i: (0, i), memory_space=pltpu.SMEM),
            pl.BlockSpec((8, tm), lambda i: (0, jnp.minimum(i + 1, n - 1)), memory_space=pltpu.SMEM),
            pl.BlockSpec((8, tm), lambda i: (0, i)),
            pl.BlockSpec((tm, d_model), lambda i: (i, 0)),
            pl.BlockSpec(memory_space=pl.ANY),
        ],
        out_specs=pl.BlockSpec((tm, d_model), lambda i: (i, 0)),
        scratch_shapes=[pltpu.VMEM((2, 2, tm, d_model), F32), pltpu.SemaphoreType.DMA((2,))],
        compiler_params=_cparams(("arbitrary",)),
        name="combine",
    )(dest, dest, gates, x1, y)


def _experts_kernel(blk_e_ref, nblk_ref, xs_ref, wg_ref, wu_ref, wd_ref, y_ref,
                    wgu_sc, wd_sc, *, d_ff):
    b = pl.program_id(0)
    active = b < nblk_ref[0]

    @pl.when(active)
    def _():
        prev = blk_e_ref[jnp.maximum(b - 1, 0)]

        @pl.when(jnp.logical_or(b == 0, blk_e_ref[b] != prev))
        def _():
            wgu_sc[:, :d_ff] = wg_ref[0, 0].astype(BF16)
            wgu_sc[:, d_ff:] = wu_ref[0, 0].astype(BF16)
            wd_sc[...] = wd_ref[0, 0].astype(BF16)

        gu = _dot(xs_ref[...].astype(BF16), wgu_sc[...])
        gate = gu[:, :d_ff]
        act = (gate * jax.nn.sigmoid(gate)) * gu[:, d_ff:]
        y_ref[...] = _dot(act.astype(BF16), wd_sc[...])

    @pl.when(jnp.logical_not(active))
    def _():
        y_ref[...] = jnp.zeros(y_ref.shape, F32)


def _experts(blk_e, nblk, xs, w_gate, w_up, w_down, *, layer, tmb):
    rows, d_model = xs.shape
    d_ff = w_gate.shape[-1]
    kern = functools.partial(_experts_kernel, d_ff=d_ff)

    def x_map(b, be, nb):
        return (jnp.minimum(b, nb[0] - 1), 0)

    def w_map(b, be, nb):
        return (layer, be[b], 0, 0)

    return pl.pallas_call(
        kern,
        out_shape=jax.ShapeDtypeStruct((rows, d_model), F32),
        grid_spec=pltpu.PrefetchScalarGridSpec(
            num_scalar_prefetch=2,
            grid=(rows // tmb,),
            in_specs=[
                pl.BlockSpec((tmb, d_model), x_map),
                pl.BlockSpec((1, 1, d_model, d_ff), w_map),
                pl.BlockSpec((1, 1, d_model, d_ff), w_map),
                pl.BlockSpec((1, 1, d_ff, d_model), w_map),
            ],
            out_specs=pl.BlockSpec((tmb, d_model), lambda b, be, nb: (b, 0)),
            scratch_shapes=[pltpu.VMEM((d_model, 2 * d_ff), BF16), pltpu.VMEM((d_ff, d_model), BF16)],
        ),
        compiler_params=_cparams(("arbitrary",)),
        name="experts",
    )(blk_e, nblk, xs, w_gate, w_up, w_down)


class _Tiles:
    def __init__(self, t, s):
        self.attn = min(256, s)
        self.proj = min(512, s)
        self.rows = min(512, t)
        self.expert = min(512, t)


def _strict_upper(n):
    a = jnp.arange(n)
    return (a[:, None] < a[None, :]).astype(BF16)


def _block_table(counts, tmb, nblk_max):
    padded = ((counts + tmb - 1) // tmb) * tmb
    pend = jnp.cumsum(padded)
    pstart = (pend - padded).astype(I32)
    blk_first = jnp.arange(nblk_max, dtype=I32) * tmb
    blk_e = jnp.minimum(jnp.sum(pend[None, :] <= blk_first[:, None], axis=1), N_EXPERTS - 1).astype(I32)
    nblk = (pend[-1] // tmb).astype(I32)
    last_e = jnp.sum(jnp.where(jnp.arange(nblk_max) == nblk - 1, blk_e, 0))
    blk_e = jnp.where(jnp.arange(nblk_max) < nblk, blk_e, last_e)
    lastblk = jnp.where(padded > 0, pend - tmb, -1).astype(I32)
    return pstart, lastblk, blk_e, nblk.reshape(1)


def kernel(x, rel_bias, ln1_g, w_in, qnorm_g, knorm_g, lambda_q1, lambda_k1, lambda_q2, lambda_k2,
           subln_g, w_branch_diff, w_branch_sb, w_out, ln2_g, w_group, w_router, w_gate, w_up, w_down):
    b, s, d_model = x.shape
    depth = w_in.shape[0]
    t = b * s
    tiles = _Tiles(t, s)
    nblk_max = 2 * t // tiles.expert + N_EXPERTS

    bias = _bias_tiles_t(rel_bias, tiles.attn) * LOG2_E
    head = jnp.arange(STACK) // HEAD_DIM
    seg = jnp.where(head[:, None] == head[None, :], 1.0 / HEAD_DIM, 0.0).astype(BF16)
    tri_attn = _strict_upper(tiles.attn)
    tri_tok = _strict_upper(tiles.proj)

    c_k, c_va = 2 * STACK, QK_COLS
    c_qs = c_va + VA_COLS
    c_ks, c_vs, c_g = c_qs + SB_COLS, c_qs + 2 * SB_COLS, c_qs + 3 * SB_COLS

    x2 = x.reshape(t, d_model)
    for l in range(depth):
        lam_init = 0.8 - 0.6 * math.exp(-0.3 * l)
        w = w_in[l].astype(BF16)
        wn = jnp.concatenate([w[:, c_k:c_va], w[:, c_ks:c_vs], w[:, c_g:]], axis=1)
        wt = jnp.transpose(jnp.concatenate(
            [w[:, :c_k], w[:, c_va:c_qs], w[:, c_qs:c_ks], w[:, c_vs:c_g]], axis=1))
        n_heads = 2 * N_DIFF_HEADS
        k_gain = jnp.tile(knorm_g[l].astype(F32), n_heads).reshape(1, 2 * STACK)
        q_gain = jnp.tile(qnorm_g[l].astype(F32) * (HEAD_DIM ** -0.5 * LOG2_E), n_heads)
        bound = _score_bound(q_gain, k_gain, bias)
        q_gain = jnp.broadcast_to(q_gain[:, None], (2 * STACK, tiles.proj))
        lamv = jnp.stack([lambda_q1[l], lambda_k1[l], lambda_q2[l], lambda_k2[l]]).astype(F32)

        nat, proj_t = _in_proj(x2, ln1_g[l].reshape(1, d_model), wn, wt, k_gain, seg, q_gain,
                               batch=b, seq=s, tm=tiles.proj, blk=tiles.attn)
        nat3 = nat.reshape(b, s, nat.shape[-1])
        oa = _diff_attn(nat3, proj_t, bound, lamv, subln_g[l].reshape(1, DIFF_V_DIM), bias,
                        tq=tiles.attn, lam_init=lam_init)
        ob = _sb_attn(nat3, proj_t, tri_attn, tq=tiles.attn)

        w_rt = jnp.concatenate([w_group[l].T, jnp.zeros((8 - N_GROUPS, d_model), F32), w_router[l].T])
        wrh, wrl = _split_bf16(w_rt.astype(F32))
        x1, xn, route, gates, counts = _post_attn(
            x2, oa.reshape(t, VA_COLS), ob.reshape(t, SB_COLS), nat,
            w_branch_diff[l].astype(BF16), w_branch_sb[l].astype(BF16), w_out[l].astype(BF16),
            ln2_g[l].reshape(1, d_model), wrh, wrl, tri_tok, tm=tiles.proj)

        pstart, lastblk, blk_e, nblk = _block_table(counts[:, 0], tiles.expert, nblk_max)
        seg_start = jnp.sum(jnp.where(route[:2, :, None] == jnp.arange(N_EXPERTS), pstart, 0), axis=-1)
        dest = jnp.concatenate([seg_start + route[2:4], jnp.zeros((6, t), I32)])
        xs = _dispatch(lastblk, nblk, dest, xn, tmb=tiles.expert, nblk_max=nblk_max, tm=tiles.rows)
        y = _experts(blk_e, nblk, xs, w_gate, w_up, w_down, layer=l, tmb=tiles.expert)
        x2 = _combine(dest, gates, x1, y, tm=tiles.rows)
    return x2.reshape(b, s, d_model)
```

```python
import functools
import math

import jax
import jax.numpy as jnp
from jax import lax
from jax.experimental import pallas as pl
from jax.experimental.pallas import tpu as pltpu

F32 = jnp.float32
BF16 = jnp.bfloat16
I32 = jnp.int32

HEAD_DIM = 64
N_DIFF_HEADS = 4
DIFF_V_DIM = 2 * HEAD_DIM
N_SB_HEADS = 8
N_BUCKETS = 32
MAX_DISTANCE = 128
N_GROUPS = 4
EXPERTS_PER_GROUP = 8
N_EXPERTS = N_GROUPS * EXPERTS_PER_GROUP
EPS = 1e-6
NEG_INF = -1e30
LOG2_E = math.log2(math.e)

V7X_LANES = 128
VMEM_LIMIT = 56 * 1024 * 1024

F32_EXP_ZERO_BELOW = -104.0

HEADS_PER_STACK = 4
STACK = HEADS_PER_STACK * HEAD_DIM

QK_COLS = 4 * N_DIFF_HEADS * HEAD_DIM
VA_COLS = N_DIFF_HEADS * DIFF_V_DIM
SB_COLS = N_SB_HEADS * HEAD_DIM
NAT_K, NAT_KS, NAT_GATE = 0, 2 * STACK, 4 * STACK
T_Q, T_VA, T_QS, T_VS = 0, 2 * STACK, 4 * STACK, 6 * STACK
T_ROWS = 8 * STACK


def _cparams(semantics, vmem=VMEM_LIMIT):
    return pltpu.CompilerParams(dimension_semantics=semantics, vmem_limit_bytes=vmem)


def _nt_dot(a, b):
    return lax.dot_general(a, b, (((1,), (1,)), ((), ())), preferred_element_type=F32)


def _dot(a, b):
    return jnp.dot(a, b, preferred_element_type=F32)


def _split_bf16(x):
    hi = x.astype(BF16)
    lo = (x - hi.astype(F32)).astype(BF16)
    return hi, lo


def _in_proj_kernel(x_ref, g_ref, wn_ref, wt_ref, kg_ref, seg_ref, qg_ref, on_ref, ot_ref,
                    *, tm, d_model, blk):
    x = x_ref[...]
    ms = jnp.mean(x * x, axis=-1, keepdims=True)
    h = ((x * lax.rsqrt(ms + EPS)) * g_ref[...]).astype(BF16)

    seg = seg_ref[...]
    for col in range(NAT_K, NAT_KS, STACK):
        acc = _dot(h, wn_ref[:, col:col + STACK])
        hi, lo = _split_bf16(acc * acc)
        msq = _dot(hi, seg) + _dot(lo, seg)
        acc = (acc * lax.rsqrt(msq + EPS)) * kg_ref[:, col:col + STACK]
        on_ref[:, col:col + STACK] = acc.astype(BF16)
    width = 2 * STACK
    on_ref[:, NAT_KS:NAT_GATE] = _dot(h, wn_ref[:, NAT_KS:NAT_GATE]).astype(BF16)
    for col in range(NAT_GATE, NAT_GATE + 2 * d_model, width):
        on_ref[:, col:col + width] = jax.nn.sigmoid(_dot(h, wn_ref[:, col:col + width])).astype(BF16)

    for row in range(0, T_ROWS, width):
        acc = _nt_dot(wt_ref[row:row + width, :], h)
        if row == T_Q:
            a3 = acc.reshape(width // HEAD_DIM, HEAD_DIM, tm)
            msq = jnp.mean(a3 * a3, axis=1, keepdims=True)
            acc = (a3 * lax.rsqrt(msq + EPS)).reshape(width, tm)
            acc = acc * qg_ref[...]
        elif row == T_QS:
            acc = acc * (HEAD_DIM ** -0.5)
        for c in range(tm // blk):
            ot_ref[0, c, row:row + width, :] = acc[:, c * blk:(c + 1) * blk].astype(BF16)


def _in_proj(x2, ln_g, wn, wt, k_gain, seg, q_gain, *, batch, seq, tm, blk):
    t, d_model = x2.shape
    nat_cols = wn.shape[1]
    tiles_per_batch = seq // tm
    kern = functools.partial(_in_proj_kernel, tm=tm, d_model=d_model, blk=blk)
    const = lambda i: (0, 0)
    return pl.pallas_call(
        kern,
        out_shape=(jax.ShapeDtypeStruct((t, nat_cols), BF16),
                   jax.ShapeDtypeStruct((batch, seq // blk, T_ROWS, blk), BF16)),
        grid=(t // tm,),
        in_specs=[
            pl.BlockSpec((tm, d_model), lambda i: (i, 0)),
            pl.BlockSpec((1, d_model), const),
            pl.BlockSpec((d_model, nat_cols), const),
            pl.BlockSpec((T_ROWS, d_model), const),
            pl.BlockSpec((1, 2 * STACK), const),
            pl.BlockSpec((STACK, STACK), const),
            pl.BlockSpec((2 * STACK, tm), const),
        ],
        out_specs=(
            pl.BlockSpec((tm, nat_cols), lambda i: (i, 0)),
            pl.BlockSpec((1, tm // blk, T_ROWS, blk),
                         lambda i: (i // tiles_per_batch, i % tiles_per_batch, 0, 0)),
        ),
        compiler_params=_cparams(("arbitrary",)),
        name="in_proj",
    )(x2, ln_g, wn, wt, k_gain, seg, q_gain)


def _stack_heads_t(q_t, qs_sc, tq):
    qf = q_t.astype(F32)
    row_head = lax.broadcasted_iota(I32, (STACK, tq), 0) // HEAD_DIM
    for h in range(HEADS_PER_STACK):
        qs_sc[:, h * tq:(h + 1) * tq] = jnp.where(row_head == h, qf, 0.0).astype(BF16)


def _t5_bucket(n):
    max_exact = N_BUCKETS // 2
    nf = jnp.maximum(n, max_exact).astype(F32)
    large = max_exact + (jnp.log(nf / max_exact) / math.log(MAX_DISTANCE / max_exact)
                         * (N_BUCKETS - max_exact)).astype(I32)
    large = jnp.minimum(large, N_BUCKETS - 1)
    return jnp.where(n < max_exact, n, large)


def _bias_tiles_t(rel_bias, tq):
    assert tq + 1 >= MAX_DISTANCE, "far blocks must lie entirely in the last bucket"
    dist = jnp.arange(tq, dtype=I32)[None, :] - jnp.arange(tq, dtype=I32)[:, None]
    rb = rel_bias.astype(F32)

    def tile(d):
        onehot = (_t5_bucket(jnp.maximum(d, 0))[..., None] == jnp.arange(N_BUCKETS)).astype(F32)
        return jnp.einsum("kqb,bh->khq", onehot, rb, precision=lax.Precision.HIGHEST)

    far = tile(jnp.full((1, 1), 2 * tq, I32))
    near = tile(dist + tq) - far
    diag = jnp.where(dist[:, None, :] >= 0, tile(dist) - far, NEG_INF)
    return jnp.stack([near, diag]).reshape(2, tq, N_DIFF_HEADS * tq)


ONES_ROWS = 16
FAR_BLOCKS = 2
BF16_NORM_MARGIN = 1.02
SAFE_EXP2_SPAN = 100.0


def _diff_attn_kernel(bound_ref, lam_ref, subg_ref, bias_ref, q_ref, k_ref, v_ref, o_ref,
                      qs_sc, m_sc, acc_sc, *, tq, lam_init):
    i = pl.program_id(1)
    for mp in range(2):
        _stack_heads_t(q_ref[0, 0, mp * STACK:(mp + 1) * STACK, :], qs_sc.at[mp], tq)
    m_sc[...] = jnp.full(m_sc.shape, NEG_INF, F32)
    acc_sc[...] = jnp.zeros(acc_sc.shape, F32)
    ones = jnp.ones((ONES_ROWS, tq), BF16)

    def step(j, nb, bias_idx, fixed_max):
        start = pl.multiple_of(j * tq, tq)
        chains = [(mp, h) for mp in range(2) for h in range(N_DIFF_HEADS)]
        scores = []
        for mp, h in chains:
            kb = k_ref[0, pl.ds(start, nb * tq), mp * STACK:(mp + 1) * STACK]
            scores.append(_dot(kb, qs_sc[mp, :, h * tq:(h + 1) * tq]))
        probs = []
        for (mp, h), s in zip(chains, scores):
            cols = slice(h * tq, (h + 1) * tq)
            if bias_idx == "diagonal":
                s = s + bias_ref[1, :, cols]
            elif bias_idx == "near+diagonal":
                s = s + bias_ref[:, :, cols].reshape(2 * tq, tq)
            if fixed_max is None:
                m_old = m_sc[mp, :, cols]
                m_new = jnp.maximum(m_old, jnp.max(s, axis=0, keepdims=True))
                m_sc[mp, :, cols] = m_new
                probs.append((jnp.exp2(m_old - m_new), jnp.exp2(s - m_new).astype(BF16)))
            else:
                probs.append((None, jnp.exp2(s - fixed_max).astype(BF16)))
        for (mp, h), (alpha, pb) in zip(chains, probs):
            rows = slice(h * DIFF_V_DIM, (h + 1) * DIFF_V_DIM)
            pv = None
            for blk in range(nb):
                v_ext = jnp.concatenate([v_ref[0, j + blk, rows, :], ones], axis=0)
                part = _dot(v_ext, pb[blk * tq:(blk + 1) * tq])
                pv = part if pv is None else pv + part
            if alpha is None:
                acc_sc[mp, h] += pv
            else:
                acc_sc[mp, h] = alpha * acc_sc[mp, h] + pv

    def sweep(fixed_max):
        n_far = jnp.maximum(i - 1, 0)

        def far_body(jj, carry):
            step(FAR_BLOCKS * jj, FAR_BLOCKS, None, fixed_max)
            return carry

        lax.fori_loop(0, n_far // FAR_BLOCKS, far_body, 0)
        for rem in range(1, FAR_BLOCKS):
            @pl.when(n_far % FAR_BLOCKS == rem)
            def _(rem=rem):
                step(n_far - rem, rem, None, fixed_max)

        @pl.when(i == 0)
        def _():
            step(i, 1, "diagonal", fixed_max)

        @pl.when(i > 0)
        def _():
            step(i - 1, 2, "near+diagonal", fixed_max)

    safe_bound = bound_ref[1] > 0.5

    @pl.when(safe_bound)
    def _():
        sweep(bound_ref[0])

    @pl.when(jnp.logical_not(safe_bound))
    def _():
        sweep(None)

    lamv = lam_ref[...]
    lam = (jnp.exp(jnp.sum(lamv[0:1] * lamv[1:2], axis=1, keepdims=True))
           - jnp.exp(jnp.sum(lamv[2:3] * lamv[3:4], axis=1, keepdims=True)) + lam_init)
    for h in range(N_DIFF_HEADS):
        rows = slice(h * DIFF_V_DIM, (h + 1) * DIFF_V_DIM)
        a1 = acc_sc[0, h]
        a2 = acc_sc[1, h]
        inv_l1 = 1.0 / a1[DIFF_V_DIM:DIFF_V_DIM + 1]
        inv_l2 = lam / a2[DIFF_V_DIM:DIFF_V_DIM + 1]
        o_t = a1[:DIFF_V_DIM] * inv_l1 - a2[:DIFF_V_DIM] * inv_l2
        o = jnp.transpose(o_t)
        ms = jnp.mean(o * o, axis=-1, keepdims=True)
        o = ((o * lax.rsqrt(ms + EPS)) * subg_ref[...]) * (1.0 - lam_init)
        o_ref[0, :, rows] = o.astype(BF16)


def _diff_attn(nat3, proj_t, bound, lamv, subln_g, bias, *, tq, lam_init):
    b, s, _ = nat3.shape
    nblk = s // tq
    m = N_DIFF_HEADS * tq
    kern = functools.partial(_diff_attn_kernel, tq=tq, lam_init=lam_init)
    two = 2 * STACK
    return pl.pallas_call(
        kern,
        out_shape=jax.ShapeDtypeStruct((b, s, VA_COLS), BF16),
        grid=(b, nblk),
        in_specs=[
            pl.BlockSpec(memory_space=pltpu.SMEM),
            pl.BlockSpec((4, HEAD_DIM), lambda bi, i: (0, 0)),
            pl.BlockSpec((1, DIFF_V_DIM), lambda bi, i: (0, 0)),
            pl.BlockSpec((2, tq, m), lambda bi, i: (0, 0, 0)),
            pl.BlockSpec((1, 1, two, tq), lambda bi, i: (bi, i, T_Q // two, 0)),
            pl.BlockSpec((1, s, two), lambda bi, i: (bi, 0, NAT_K // two)),
            pl.BlockSpec((1, nblk, two, tq), lambda bi, i: (bi, 0, T_VA // two, 0)),
        ],
        out_specs=pl.BlockSpec((1, tq, VA_COLS), lambda bi, i: (bi, i, 0)),
        scratch_shapes=[
            pltpu.VMEM((2, STACK, m), BF16),
            pltpu.VMEM((2, 1, m), F32),
            pltpu.VMEM((2, N_DIFF_HEADS, DIFF_V_DIM + ONES_ROWS, tq), F32),
        ],
        compiler_params=_cparams(("arbitrary", "arbitrary")),
        name="diff_attn",
    )(bound, lamv, subln_g, bias, proj_t, nat3, proj_t)


def _score_bound(q_gain, k_gain, bias):
    qk = HEAD_DIM * jnp.max(jnp.abs(q_gain)) * jnp.max(jnp.abs(k_gain)) * BF16_NORM_MARGIN
    finite = bias > 0.5 * NEG_INF
    b_hi = jnp.maximum(jnp.max(jnp.where(finite, bias, 0.0)), 0.0)
    b_lo = jnp.minimum(jnp.min(jnp.where(finite, bias, 0.0)), 0.0)
    bound = qk + b_hi
    safe = (2.0 * qk + (b_hi - b_lo)) <= SAFE_EXP2_SPAN
    return jnp.stack([bound, safe.astype(F32)]).astype(F32)


def _sb_attn_kernel(tri_ref, q_ref, k_ref, v_ref, o_ref, qs_sc, acc_sc, c_sc, *, tq):
    i = pl.program_id(2)
    m = HEADS_PER_STACK * tq
    _stack_heads_t(q_ref[0, 0], qs_sc, tq)
    acc_sc[...] = jnp.zeros(acc_sc.shape, F32)
    c_sc[...] = jnp.zeros(c_sc.shape, F32)

    def step(blocks):
        mask = (lax.broadcasted_iota(I32, (tq, tq), 0)
                < lax.broadcasted_iota(I32, (tq, tq), 1))
        chains = [(j, diag, h) for j, diag in blocks for h in range(HEADS_PER_STACK)]
        tri = tri_ref[...]
        zs = []
        for j, _, h in chains:
            kb = k_ref[0, pl.ds(pl.multiple_of(j * tq, tq), tq), :]
            zs.append(_dot(kb, qs_sc[:, h * tq:(h + 1) * tq]))
        parts = []
        for (j, diag, h), z in zip(chains, zs):
            nz = -z
            log_1m = jnp.minimum(nz, 0.0) - jnp.log(1.0 + jnp.exp(jnp.minimum(z, nz)))
            log_sig = z + log_1m
            if diag:
                log_1m = jnp.where(mask, log_1m, 0.0)
            parts.append((log_sig, log_1m) + _split_bf16(log_1m))
        sufs = [_dot(tri, hi) + _dot(tri, lo) for _, _, hi, lo in parts]
        c = [c_sc[:, h * tq:(h + 1) * tq] for h in range(HEADS_PER_STACK)]
        weights = []
        for (j, diag, h), (log_sig, log_1m, _, _), suf in zip(chains, parts, sufs):
            a = jnp.exp(log_sig + (suf + c[h]))
            if diag:
                a = jnp.where(mask, a, 0.0)
            c[h] = c[h] + jnp.sum(log_1m, axis=0, keepdims=True)
            weights.append(a.astype(BF16))
        for h in range(HEADS_PER_STACK):
            c_sc[:, h * tq:(h + 1) * tq] = c[h]
        for (j, _, h), a in zip(chains, weights):
            rows = slice(h * HEAD_DIM, (h + 1) * HEAD_DIM)
            acc_sc[rows, :] += _dot(v_ref[0, j, rows, :], a)

    @pl.when(i == 0)
    def _():
        step([(i, True)])

    @pl.when(i > 0)
    def _():
        step([(i, True), (i - 1, False)])

    def cond(j):
        return jnp.logical_and(j >= 0, jnp.max(c_sc[...]) > F32_EXP_ZERO_BELOW)

    def body(j):
        step([(j, False)])
        return j - 1

    lax.while_loop(cond, body, i - 2)

    o_ref[0] = jnp.transpose(acc_sc[...]).astype(BF16)


def _sb_attn(nat3, proj_t, tri, *, tq):
    b, s, _ = nat3.shape
    nblk = s // tq
    n_stacks = SB_COLS // STACK
    m = HEADS_PER_STACK * tq
    kern = functools.partial(_sb_attn_kernel, tq=tq)
    return pl.pallas_call(
        kern,
        out_shape=jax.ShapeDtypeStruct((b, s, SB_COLS), BF16),
        grid=(b, n_stacks, nblk),
        in_specs=[
            pl.BlockSpec((tq, tq), lambda bi, g, i: (0, 0)),
            pl.BlockSpec((1, 1, STACK, tq), lambda bi, g, i: (bi, i, T_QS // STACK + g, 0)),
            pl.BlockSpec((1, s, STACK), lambda bi, g, i: (bi, 0, NAT_KS // STACK + g)),
            pl.BlockSpec((1, nblk, STACK, tq), lambda bi, g, i: (bi, 0, T_VS // STACK + g, 0)),
        ],
        out_specs=pl.BlockSpec((1, tq, STACK), lambda bi, g, i: (bi, i, g)),
        scratch_shapes=[pltpu.VMEM((STACK, m), BF16), pltpu.VMEM((STACK, tq), F32),
                        pltpu.VMEM((1, m), F32)],
        compiler_params=_cparams(("arbitrary", "arbitrary", "arbitrary")),
        name="sb_attn",
    )(tri, proj_t, nat3, proj_t)


ROUTER_ROWS = 8 + N_EXPERTS


def _post_attn_kernel(x_ref, oa_ref, ob_ref, ga_ref, gb_ref, wbd_ref, wbs_ref, wo_ref, g_ref,
                      wrh_ref, wrl_ref, tri_ref,
                      x1_ref, xn_ref, route_ref, gates_ref, counts_ref, cnt_sc, *, tm, d_model):
    @pl.when(pl.program_id(0) == 0)
    def _():
        cnt_sc[...] = jnp.zeros(cnt_sc.shape, F32)

    ga = ga_ref[...].astype(F32)
    gb = gb_ref[...].astype(F32)
    mixed = ga * _dot(oa_ref[...], wbd_ref[...]) + gb * _dot(ob_ref[...], wbs_ref[...])
    x1 = x_ref[...] + _dot(mixed.astype(BF16), wo_ref[...])
    x1_ref[...] = x1
    ms = jnp.mean(x1 * x1, axis=-1, keepdims=True)
    xn = (x1 * lax.rsqrt(ms + EPS)) * g_ref[...]
    xn_ref[...] = xn

    xh, xl = _split_bf16(xn)
    wrh = wrh_ref[...]
    logits = _nt_dot(wrh, xh) + _nt_dot(wrh, xl) + _nt_dot(wrl_ref[...], xh)

    gl = [logits[r:r + 1] for r in range(N_GROUPS)]
    gmax = functools.reduce(jnp.maximum, gl)
    grp = jnp.full((1, tm), N_GROUPS - 1, I32)
    for r in range(N_GROUPS - 2, -1, -1):
        grp = jnp.where(gl[r] == gmax, r, grp)
    pg = 1.0 / functools.reduce(lambda a, b: a + b, [jnp.exp(v - gmax) for v in gl])

    el = logits[8:8 + EXPERTS_PER_GROUP]
    for r in range(1, N_GROUPS):
        lo = 8 + r * EXPERTS_PER_GROUP
        el = jnp.where(grp == r, logits[lo:lo + EXPERTS_PER_GROUP], el)
    ex = jnp.exp(el - jnp.max(el, axis=0, keepdims=True))
    prob = ex / jnp.sum(ex, axis=0, keepdims=True)
    sub = lax.broadcasted_iota(I32, (EXPERTS_PER_GROUP, tm), 0).astype(F32)
    none = float(EXPERTS_PER_GROUP)
    v1 = jnp.max(prob, axis=0, keepdims=True)
    i1 = jnp.min(jnp.where(prob == v1, sub, none), axis=0, keepdims=True)
    rest = jnp.where(sub == i1, -1.0, prob)
    v2 = jnp.max(rest, axis=0, keepdims=True)
    i2 = jnp.min(jnp.where(rest == v2, sub, none), axis=0, keepdims=True)
    denom = v1 + v2
    gate1 = pg * v1 / denom
    gate2 = pg * v2 / denom
    e1 = grp * EXPERTS_PER_GROUP + i1.astype(I32)
    e2 = grp * EXPERTS_PER_GROUP + i2.astype(I32)

    eio = lax.broadcasted_iota(I32, (N_EXPERTS, tm), 0)
    hit1 = eio == e1
    hit2 = eio == e2
    onehot = jnp.where(hit1, 1.0, 0.0) + jnp.where(hit2, 1.0, 0.0)
    prefix = _dot(onehot.astype(BF16), tri_ref[...]) + cnt_sc[...]
    r1 = jnp.sum(jnp.where(hit1, prefix, 0.0), axis=0, keepdims=True)
    r2 = jnp.sum(jnp.where(hit2, prefix, 0.0), axis=0, keepdims=True)
    cnt = cnt_sc[...] + jnp.sum(onehot, axis=1, keepdims=True)
    cnt_sc[...] = cnt
    counts_ref[...] = jnp.broadcast_to(cnt, counts_ref.shape).astype(I32)

    row = lax.broadcasted_iota(I32, (8, tm), 0)
    route_ref[...] = jnp.where(row == 0, e1, jnp.where(row == 1, e2, jnp.where(
        row == 2, r1.astype(I32), jnp.where(row == 3, r2.astype(I32), 0))))
    gates_ref[...] = jnp.where(row == 0, gate1, jnp.where(row == 1, gate2, 0.0))


def _post_attn(x2, oa, ob, nat, wbd, wbs, wo, ln_g, wrh, wrl, tri, *, tm):
    t, d_model = x2.shape
    kern = functools.partial(_post_attn_kernel, tm=tm, d_model=d_model)
    gate_blk = NAT_GATE // d_model
    assert gate_blk * d_model == NAT_GATE
    const = lambda i: (0, 0)
    return pl.pallas_call(
        kern,
        out_shape=(
            jax.ShapeDtypeStruct((t, d_model), F32),
            jax.ShapeDtypeStruct((t, d_model), F32),
            jax.ShapeDtypeStruct((8, t), I32),
            jax.ShapeDtypeStruct((8, t), F32),
            jax.ShapeDtypeStruct((N_EXPERTS, V7X_LANES), I32),
        ),
        grid=(t // tm,),
        in_specs=[
            pl.BlockSpec((tm, d_model), lambda i: (i, 0)),
            pl.BlockSpec((tm, VA_COLS), lambda i: (i, 0)),
            pl.BlockSpec((tm, SB_COLS), lambda i: (i, 0)),
            pl.BlockSpec((tm, d_model), lambda i: (i, gate_blk)),
            pl.BlockSpec((tm, d_model), lambda i: (i, gate_blk + 1)),
            pl.BlockSpec((VA_COLS, d_model), const),
            pl.BlockSpec((SB_COLS, d_model), const),
            pl.BlockSpec((d_model, d_model), const),
            pl.BlockSpec((1, d_model), const),
            pl.BlockSpec((ROUTER_ROWS, d_model), const),
            pl.BlockSpec((ROUTER_ROWS, d_model), const),
            pl.BlockSpec((tm, tm), const),
        ],
        out_specs=(
            pl.BlockSpec((tm, d_model), lambda i: (i, 0)),
            pl.BlockSpec((tm, d_model), lambda i: (i, 0)),
            pl.BlockSpec((8, tm), lambda i: (0, i)),
            pl.BlockSpec((8, tm), lambda i: (0, i)),
            pl.BlockSpec((N_EXPERTS, V7X_LANES), const),
        ),
        scratch_shapes=[pltpu.VMEM((N_EXPERTS, 1), F32)],
        compiler_params=_cparams(("arbitrary",)),
        name="post_attn",
    )(x2, oa, ob, nat, nat, wbd, wbs, wo, ln_g, wrh, wrl, tri)


ROW_UNROLL = 8


def _row_copy(src, src_row, dst, dst_row, sem):
    return pltpu.make_async_copy(src.at[pl.ds(src_row, 1)], dst.at[pl.ds(dst_row, 1)], sem)


def _dispatch_kernel(lastblk_ref, nblk_ref, dest_ref, xn_ref, xs_ref, zero_sc, sem, zsem,
                     *, tm, tmb, nblk_max):
    @pl.when(pl.program_id(0) == 0)
    def _():
        zero_sc[...] = jnp.zeros(zero_sc.shape, F32)

        def zero_block(row):
            return pltpu.make_async_copy(
                zero_sc, xs_ref.at[pl.ds(pl.multiple_of(row, tmb), tmb)], zsem)

        for go in (lambda c: c.start(), lambda c: c.wait()):
            def seg_block(e, carry, go=go):
                @pl.when(lastblk_ref[e] >= 0)
                def _():
                    go(zero_block(lastblk_ref[e]))
                return carry

            def tail_block(b, carry, go=go):
                go(zero_block(b * tmb))
                return carry

            lax.fori_loop(0, N_EXPERTS, seg_block, 0)
            lax.fori_loop(nblk_ref[0], nblk_max, tail_block, 0)

    def start(r, carry):
        for k in range(2):
            _row_copy(xn_ref, r, xs_ref, dest_ref[k, r], sem).start(priority=k)
        return carry

    lax.fori_loop(0, tm, start, 0, unroll=ROW_UNROLL)

    def wait(r, carry):
        for k in range(2):
            _row_copy(xn_ref, 0, xs_ref, 0, sem).wait()
        return carry

    lax.fori_loop(0, tm, wait, 0, unroll=ROW_UNROLL)


def _dispatch(lastblk, nblk, dest, xn, *, tmb, nblk_max, tm):
    t, d_model = xn.shape
    kern = functools.partial(_dispatch_kernel, tm=tm, tmb=tmb, nblk_max=nblk_max)
    return pl.pallas_call(
        kern,
        out_shape=jax.ShapeDtypeStruct((nblk_max * tmb, d_model), F32),
        grid_spec=pltpu.PrefetchScalarGridSpec(
            num_scalar_prefetch=2,
            grid=(t // tm,),
            in_specs=[
                pl.BlockSpec((8, tm), lambda i, *_: (0, i), memory_space=pltpu.SMEM),
                pl.BlockSpec((tm, d_model), lambda i, *_: (i, 0)),
            ],
            out_specs=pl.BlockSpec(memory_space=pl.ANY),
            scratch_shapes=[pltpu.VMEM((tmb, d_model), F32), pltpu.SemaphoreType.DMA,
                            pltpu.SemaphoreType.DMA],
        ),
        compiler_params=_cparams(("arbitrary",)),
        name="dispatch",
    )(lastblk, nblk, dest, xn)


def _combine_kernel(dest_ref, dest_next_ref, gates_ref, x1_ref, y_ref, o_ref, buf, sems, *, tm):
    i = pl.program_id(0)
    slot = i % 2

    def issue(idx_ref, to_slot):
        for r in range(tm):
            for k in range(2):
                _row_copy(y_ref, idx_ref[k, r], buf.at[to_slot, k], r, sems.at[to_slot]).start(priority=k)

    @pl.when(i == 0)
    def _():
        issue(dest_ref, 0)

    @pl.when(i + 1 < pl.num_programs(0))
    def _():
        issue(dest_next_ref, 1 - slot)

    def wait(r, carry):
        for k in range(2):
            _row_copy(y_ref, 0, buf.at[slot, k], 0, sems.at[slot]).wait()
        return carry

    lax.fori_loop(0, tm, wait, 0, unroll=ROW_UNROLL)

    gt = jnp.transpose(gates_ref[...])
    o_ref[...] = x1_ref[...] + gt[:, 0:1] * buf[slot, 0] + gt[:, 1:2] * buf[slot, 1]


def _combine(dest, gates, x1, y, *, tm):
    t, d_model = x1.shape
    n = t // tm
    kern = functools.partial(_combine_kernel, tm=tm)
    return pl.pallas_call(
        kern,
        out_shape=jax.ShapeDtypeStruct((t, d_model), F32),
        grid=(n,),
        in_specs=[
            pl.BlockSpec((8, tm), lambda i: (0, i), memory_space=pltpu.SMEM),
            pl.BlockSpec((8, tm), lambda i: (0, jnp.minimum(i + 1, n - 1)), memory_space=pltpu.SMEM),
            pl.BlockSpec((8, tm), lambda i: (0, i)),
            pl.BlockSpec((tm, d_model), lambda i: (i, 0)),
            pl.BlockSpec(memory_space=pl.ANY),
        ],
        out_specs=pl.BlockSpec((tm, d_model), lambda i: (i, 0)),
        scratch_shapes=[pltpu.VMEM((2, 2, tm, d_model), F32), pltpu.SemaphoreType.DMA((2,))],
        compiler_params=_cparams(("arbitrary",)),
        name="combine",
    )(dest, dest, gates, x1, y)


def _experts_kernel(blk_e_ref, nblk_ref, xs_ref, wg_ref, wu_ref, wd_ref, y_ref,
                    wgu_sc, wd_sc, *, d_ff):
    b = pl.program_id(0)
    active = b < nblk_ref[0]

    @pl.when(active)
    def _():
        prev = blk_e_ref[jnp.maximum(b - 1, 0)]

        @pl.when(jnp.logical_or(b == 0, blk_e_ref[b] != prev))
        def _():
            wgu_sc[:, :d_ff] = wg_ref[0, 0].astype(BF16)
            wgu_sc[:, d_ff:] = wu_ref[0, 0].astype(BF16)
            wd_sc[...] = wd_ref[0, 0].astype(BF16)

        gu = _dot(xs_ref[...].astype(BF16), wgu_sc[...])
        gate = gu[:, :d_ff]
        act = (gate * jax.nn.sigmoid(gate)) * gu[:, d_ff:]
        y_ref[...] = _dot(act.astype(BF16), wd_sc[...])

    @pl.when(jnp.logical_not(active))
    def _():
        y_ref[...] = jnp.zeros(y_ref.shape, F32)


def _experts(blk_e, nblk, xs, w_gate, w_up, w_down, *, layer, tmb):
    rows, d_model = xs.shape
    d_ff = w_gate.shape[-1]
    kern = functools.partial(_experts_kernel, d_ff=d_ff)

    def x_map(b, be, nb):
        return (jnp.minimum(b, nb[0] - 1), 0)

    def w_map(b, be, nb):
        return (layer, be[b], 0, 0)

    return pl.pallas_call(
        kern,
        out_shape=jax.ShapeDtypeStruct((rows, d_model), F32),
        grid_spec=pltpu.PrefetchScalarGridSpec(
            num_scalar_prefetch=2,
            grid=(rows // tmb,),
            in_specs=[
                pl.BlockSpec((tmb, d_model), x_map),
                pl.BlockSpec((1, 1, d_model, d_ff), w_map),
                pl.BlockSpec((1, 1, d_model, d_ff), w_map),
                pl.BlockSpec((1, 1, d_ff, d_model), w_map),
            ],
            out_specs=pl.BlockSpec((tmb, d_model), lambda b, be, nb: (b, 0)),
            scratch_shapes=[pltpu.VMEM((d_model, 2 * d_ff), BF16), pltpu.VMEM((d_ff, d_model), BF16)],
        ),
        compiler_params=_cparams(("arbitrary",)),
        name="experts",
    )(blk_e, nblk, xs, w_gate, w_up, w_down)


class _Tiles:
    def __init__(self, t, s):
        self.attn = min(256, s)
        self.proj = min(512, s)
        self.rows = min(512, t)
        self.combine = min(256, t)
        self.expert = min(512, t)


def _strict_upper(n):
    a = jnp.arange(n)
    return (a[:, None] < a[None, :]).astype(BF16)


def _block_table(counts, tmb, nblk_max):
    padded = ((counts + tmb - 1) // tmb) * tmb
    pend = jnp.cumsum(padded)
    pstart = (pend - padded).astype(I32)
    blk_first = jnp.arange(nblk_max, dtype=I32) * tmb
    blk_e = jnp.minimum(jnp.sum(pend[None, :] <= blk_first[:, None], axis=1), N_EXPERTS - 1).astype(I32)
    nblk = (pend[-1] // tmb).astype(I32)
    last_e = jnp.sum(jnp.where(jnp.arange(nblk_max) == nblk - 1, blk_e, 0))
    blk_e = jnp.where(jnp.arange(nblk_max) < nblk, blk_e, last_e)
    lastblk = jnp.where(padded > 0, pend - tmb, -1).astype(I32)
    return pstart, lastblk, blk_e, nblk.reshape(1)


def kernel(x, rel_bias, ln1_g, w_in, qnorm_g, knorm_g, lambda_q1, lambda_k1, lambda_q2, lambda_k2,
           subln_g, w_branch_diff, w_branch_sb, w_out, ln2_g, w_group, w_router, w_gate, w_up, w_down):
    b, s, d_model = x.shape
    depth = w_in.shape[0]
    t = b * s
    tiles = _Tiles(t, s)
    nblk_max = 2 * t // tiles.expert + N_EXPERTS

    bias = _bias_tiles_t(rel_bias, tiles.attn) * LOG2_E
    head = jnp.arange(STACK) // HEAD_DIM
    seg = jnp.where(head[:, None] == head[None, :], 1.0 / HEAD_DIM, 0.0).astype(BF16)
    tri_attn = _strict_upper(tiles.attn)
    tri_tok = _strict_upper(tiles.proj)

    c_k, c_va = 2 * STACK, QK_COLS
    c_qs = c_va + VA_COLS
    c_ks, c_vs, c_g = c_qs + SB_COLS, c_qs + 2 * SB_COLS, c_qs + 3 * SB_COLS

    x2 = x.reshape(t, d_model)
    for l in range(depth):
        lam_init = 0.8 - 0.6 * math.exp(-0.3 * l)
        w = w_in[l].astype(BF16)
        wn = jnp.concatenate([w[:, c_k:c_va], w[:, c_ks:c_vs], w[:, c_g:]], axis=1)
        wt = jnp.transpose(jnp.concatenate(
            [w[:, :c_k], w[:, c_va:c_qs], w[:, c_qs:c_ks], w[:, c_vs:c_g]], axis=1))
        n_heads = 2 * N_DIFF_HEADS
        k_gain = jnp.tile(knorm_g[l].astype(F32), n_heads).reshape(1, 2 * STACK)
        q_gain = jnp.tile(qnorm_g[l].astype(F32) * (HEAD_DIM ** -0.5 * LOG2_E), n_heads)
        bound = _score_bound(q_gain, k_gain, bias)
        q_gain = jnp.broadcast_to(q_gain[:, None], (2 * STACK, tiles.proj))
        lamv = jnp.stack([lambda_q1[l], lambda_k1[l], lambda_q2[l], lambda_k2[l]]).astype(F32)

        nat, proj_t = _in_proj(x2, ln1_g[l].reshape(1, d_model), wn, wt, k_gain, seg, q_gain,
                               batch=b, seq=s, tm=tiles.proj, blk=tiles.attn)
        nat3 = nat.reshape(b, s, nat.shape[-1])
        oa = _diff_attn(nat3, proj_t, bound, lamv, subln_g[l].reshape(1, DIFF_V_DIM), bias,
                        tq=tiles.attn, lam_init=lam_init)
        ob = _sb_attn(nat3, proj_t, tri_attn, tq=tiles.attn)

        w_rt = jnp.concatenate([w_group[l].T, jnp.zeros((8 - N_GROUPS, d_model), F32), w_router[l].T])
        wrh, wrl = _split_bf16(w_rt.astype(F32))
        x1, xn, route, gates, counts = _post_attn(
            x2, oa.reshape(t, VA_COLS), ob.reshape(t, SB_COLS), nat,
            w_branch_diff[l].astype(BF16), w_branch_sb[l].astype(BF16), w_out[l].astype(BF16),
            ln2_g[l].reshape(1, d_model), wrh, wrl, tri_tok, tm=tiles.proj)

        pstart, lastblk, blk_e, nblk = _block_table(counts[:, 0], tiles.expert, nblk_max)
        seg_start = jnp.sum(jnp.where(route[:2, :, None] == jnp.arange(N_EXPERTS), pstart, 0), axis=-1)
        dest = jnp.concatenate([seg_start + route[2:4], jnp.zeros((6, t), I32)])
        xs = _dispatch(lastblk, nblk, dest, xn, tmb=tiles.expert, nblk_max=nblk_max, tm=tiles.rows)
        y = _experts(blk_e, nblk, xs, w_gate, w_up, w_down, layer=l, tmb=tiles.expert)
        x2 = _combine(dest, gates, x1, y, tm=tiles.combine)
    return x2.reshape(b, s, d_model)
```

```python
import functools
import math

import jax
import jax.numpy as jnp
from jax import lax
from jax.experimental import pallas as pl
from jax.experimental.pallas import tpu as pltpu

F32 = jnp.float32
BF16 = jnp.bfloat16
I32 = jnp.int32

HEAD_DIM = 64
N_DIFF_HEADS = 4
DIFF_V_DIM = 2 * HEAD_DIM
N_SB_HEADS = 8
N_BUCKETS = 32
MAX_DISTANCE = 128
N_GROUPS = 4
EXPERTS_PER_GROUP = 8
N_EXPERTS = N_GROUPS * EXPERTS_PER_GROUP
EPS = 1e-6
NEG_INF = -1e30
LOG2_E = math.log2(math.e)

V7X_LANES = 128
VMEM_LIMIT = 56 * 1024 * 1024

F32_EXP_ZERO_BELOW = -104.0

HEADS_PER_STACK = 4
STACK = HEADS_PER_STACK * HEAD_DIM

QK_COLS = 4 * N_DIFF_HEADS * HEAD_DIM
VA_COLS = N_DIFF_HEADS * DIFF_V_DIM
SB_COLS = N_SB_HEADS * HEAD_DIM
NAT_K, NAT_KS, NAT_GATE = 0, 2 * STACK, 4 * STACK
T_Q, T_VA, T_QS, T_VS = 0, 2 * STACK, 4 * STACK, 6 * STACK
T_ROWS = 8 * STACK


def _cparams(semantics, vmem=VMEM_LIMIT):
    return pltpu.CompilerParams(dimension_semantics=semantics, vmem_limit_bytes=vmem)


def _nt_dot(a, b):
    return lax.dot_general(a, b, (((1,), (1,)), ((), ())), preferred_element_type=F32)


def _dot(a, b):
    return jnp.dot(a, b, preferred_element_type=F32)


def _split_bf16(x):
    hi = x.astype(BF16)
    lo = (x - hi.astype(F32)).astype(BF16)
    return hi, lo


def _in_proj_kernel(x_ref, g_ref, wn_ref, wt_ref, kg_ref, seg_ref, qg_ref, on_ref, ot_ref,
                    *, tm, d_model, blk):
    x = x_ref[...]
    ms = jnp.mean(x * x, axis=-1, keepdims=True)
    h = ((x * lax.rsqrt(ms + EPS)) * g_ref[...]).astype(BF16)

    seg = seg_ref[...]
    for col in range(NAT_K, NAT_KS, STACK):
        acc = _dot(h, wn_ref[:, col:col + STACK])
        hi, lo = _split_bf16(acc * acc)
        msq = _dot(hi, seg) + _dot(lo, seg)
        acc = (acc * lax.rsqrt(msq + EPS)) * kg_ref[:, col:col + STACK]
        on_ref[:, col:col + STACK] = acc.astype(BF16)
    width = 2 * STACK
    on_ref[:, NAT_KS:NAT_GATE] = _dot(h, wn_ref[:, NAT_KS:NAT_GATE]).astype(BF16)
    for col in range(NAT_GATE, NAT_GATE + 2 * d_model, width):
        on_ref[:, col:col + width] = jax.nn.sigmoid(_dot(h, wn_ref[:, col:col + width])).astype(BF16)

    for row in range(0, T_ROWS, width):
        acc = _nt_dot(wt_ref[row:row + width, :], h)
        if row == T_Q:
            a3 = acc.reshape(width // HEAD_DIM, HEAD_DIM, tm)
            msq = jnp.mean(a3 * a3, axis=1, keepdims=True)
            acc = (a3 * lax.rsqrt(msq + EPS)).reshape(width, tm)
            acc = acc * qg_ref[...]
        elif row == T_QS:
            acc = acc * (HEAD_DIM ** -0.5)
        for c in range(tm // blk):
            ot_ref[0, c, row:row + width, :] = acc[:, c * blk:(c + 1) * blk].astype(BF16)


def _in_proj(x2, ln_g, wn, wt, k_gain, seg, q_gain, *, batch, seq, tm, blk):
    t, d_model = x2.shape
    nat_cols = wn.shape[1]
    tiles_per_batch = seq // tm
    kern = functools.partial(_in_proj_kernel, tm=tm, d_model=d_model, blk=blk)
    const = lambda i: (0, 0)
    return pl.pallas_call(
        kern,
        out_shape=(jax.ShapeDtypeStruct((t, nat_cols), BF16),
                   jax.ShapeDtypeStruct((batch, seq // blk, T_ROWS, blk), BF16)),
        grid=(t // tm,),
        in_specs=[
            pl.BlockSpec((tm, d_model), lambda i: (i, 0)),
            pl.BlockSpec((1, d_model), const),
            pl.BlockSpec((d_model, nat_cols), const),
            pl.BlockSpec((T_ROWS, d_model), const),
            pl.BlockSpec((1, 2 * STACK), const),
            pl.BlockSpec((STACK, STACK), const),
            pl.BlockSpec((2 * STACK, tm), const),
        ],
        out_specs=(
            pl.BlockSpec((tm, nat_cols), lambda i: (i, 0)),
            pl.BlockSpec((1, tm // blk, T_ROWS, blk),
                         lambda i: (i // tiles_per_batch, i % tiles_per_batch, 0, 0)),
        ),
        compiler_params=_cparams(("arbitrary",)),
        name="in_proj",
    )(x2, ln_g, wn, wt, k_gain, seg, q_gain)


def _stack_heads_t(q_t, qs_sc, tq):
    qf = q_t.astype(F32)
    row_head = lax.broadcasted_iota(I32, (STACK, tq), 0) // HEAD_DIM
    for h in range(HEADS_PER_STACK):
        qs_sc[:, h * tq:(h + 1) * tq] = jnp.where(row_head == h, qf, 0.0).astype(BF16)


def _t5_bucket(n):
    max_exact = N_BUCKETS // 2
    nf = jnp.maximum(n, max_exact).astype(F32)
    large = max_exact + (jnp.log(nf / max_exact) / math.log(MAX_DISTANCE / max_exact)
                         * (N_BUCKETS - max_exact)).astype(I32)
    large = jnp.minimum(large, N_BUCKETS - 1)
    return jnp.where(n < max_exact, n, large)


def _bias_tiles_t(rel_bias, tq):
    assert tq + 1 >= MAX_DISTANCE, "far blocks must lie entirely in the last bucket"
    dist = jnp.arange(tq, dtype=I32)[None, :] - jnp.arange(tq, dtype=I32)[:, None]
    rb = rel_bias.astype(F32)

    def tile(d):
        onehot = (_t5_bucket(jnp.maximum(d, 0))[..., None] == jnp.arange(N_BUCKETS)).astype(F32)
        return jnp.einsum("kqb,bh->khq", onehot, rb, precision=lax.Precision.HIGHEST)

    far = tile(jnp.full((1, 1), 2 * tq, I32))
    near = tile(dist + tq) - far
    diag = jnp.where(dist[:, None, :] >= 0, tile(dist) - far, NEG_INF)
    return jnp.stack([near, diag]).reshape(2, tq, N_DIFF_HEADS * tq)


ONES_ROWS = 16
FAR_BLOCKS = 2
BF16_NORM_MARGIN = 1.02
SAFE_EXP2_SPAN = 100.0


def _diff_attn_kernel(bound_ref, lam_ref, subg_ref, bias_ref, q_ref, k_ref, v_ref, o_ref,
                      qs_sc, m_sc, acc_sc, *, tq, lam_init):
    i = pl.program_id(1)
    for mp in range(2):
        _stack_heads_t(q_ref[0, 0, mp * STACK:(mp + 1) * STACK, :], qs_sc.at[mp], tq)
    m_sc[...] = jnp.full(m_sc.shape, NEG_INF, F32)
    acc_sc[...] = jnp.zeros(acc_sc.shape, F32)
    ones = jnp.ones((ONES_ROWS, tq), BF16)

    def step(j, nb, bias_idx, fixed_max):
        start = pl.multiple_of(j * tq, tq)
        chains = [(mp, h) for mp in range(2) for h in range(N_DIFF_HEADS)]
        scores = []
        for mp, h in chains:
            kb = k_ref[0, pl.ds(start, nb * tq), mp * STACK:(mp + 1) * STACK]
            scores.append(_dot(kb, qs_sc[mp, :, h * tq:(h + 1) * tq]))
        probs = []
        for (mp, h), s in zip(chains, scores):
            cols = slice(h * tq, (h + 1) * tq)
            if bias_idx == "diagonal":
                s = s + bias_ref[1, :, cols]
            elif bias_idx == "near+diagonal":
                s = s + bias_ref[:, :, cols].reshape(2 * tq, tq)
            if fixed_max is None:
                m_old = m_sc[mp, :, cols]
                m_new = jnp.maximum(m_old, jnp.max(s, axis=0, keepdims=True))
                m_sc[mp, :, cols] = m_new
                probs.append((jnp.exp2(m_old - m_new), jnp.exp2(s - m_new).astype(BF16)))
            else:
                probs.append((None, jnp.exp2(s - fixed_max).astype(BF16)))
        for (mp, h), (alpha, pb) in zip(chains, probs):
            rows = slice(h * DIFF_V_DIM, (h + 1) * DIFF_V_DIM)
            pv = None
            for blk in range(nb):
                v_ext = jnp.concatenate([v_ref[0, j + blk, rows, :], ones], axis=0)
                part = _dot(v_ext, pb[blk * tq:(blk + 1) * tq])
                pv = part if pv is None else pv + part
            if alpha is None:
                acc_sc[mp, h] += pv
            else:
                acc_sc[mp, h] = alpha * acc_sc[mp, h] + pv

    def sweep(fixed_max):
        n_far = jnp.maximum(i - 1, 0)

        def far_body(jj, carry):
            step(FAR_BLOCKS * jj, FAR_BLOCKS, None, fixed_max)
            return carry

        lax.fori_loop(0, n_far // FAR_BLOCKS, far_body, 0)
        for rem in range(1, FAR_BLOCKS):
            @pl.when(n_far % FAR_BLOCKS == rem)
            def _(rem=rem):
                step(n_far - rem, rem, None, fixed_max)

        @pl.when(i == 0)
        def _():
            step(i, 1, "diagonal", fixed_max)

        @pl.when(i > 0)
        def _():
            step(i - 1, 2, "near+diagonal", fixed_max)

    safe_bound = bound_ref[1] > 0.5

    @pl.when(safe_bound)
    def _():
        sweep(bound_ref[0])

    @pl.when(jnp.logical_not(safe_bound))
    def _():
        sweep(None)

    lamv = lam_ref[...]
    lam = (jnp.exp(jnp.sum(lamv[0:1] * lamv[1:2], axis=1, keepdims=True))
           - jnp.exp(jnp.sum(lamv[2:3] * lamv[3:4], axis=1, keepdims=True)) + lam_init)
    for h in range(N_DIFF_HEADS):
        rows = slice(h * DIFF_V_DIM, (h + 1) * DIFF_V_DIM)
        a1 = acc_sc[0, h]
        a2 = acc_sc[1, h]
        inv_l1 = 1.0 / a1[DIFF_V_DIM:DIFF_V_DIM + 1]
        inv_l2 = lam / a2[DIFF_V_DIM:DIFF_V_DIM + 1]
        o_t = a1[:DIFF_V_DIM] * inv_l1 - a2[:DIFF_V_DIM] * inv_l2
        o = jnp.transpose(o_t)
        ms = jnp.mean(o * o, axis=-1, keepdims=True)
        o = ((o * lax.rsqrt(ms + EPS)) * subg_ref[...]) * (1.0 - lam_init)
        o_ref[0, :, rows] = o.astype(BF16)


def _diff_attn(nat3, proj_t, bound, lamv, subln_g, bias, *, tq, lam_init):
    b, s, _ = nat3.shape
    nblk = s // tq
    m = N_DIFF_HEADS * tq
    kern = functools.partial(_diff_attn_kernel, tq=tq, lam_init=lam_init)
    two = 2 * STACK
    return pl.pallas_call(
        kern,
        out_shape=jax.ShapeDtypeStruct((b, s, VA_COLS), BF16),
        grid=(b, nblk),
        in_specs=[
            pl.BlockSpec(memory_space=pltpu.SMEM),
            pl.BlockSpec((4, HEAD_DIM), lambda bi, i: (0, 0)),
            pl.BlockSpec((1, DIFF_V_DIM), lambda bi, i: (0, 0)),
            pl.BlockSpec((2, tq, m), lambda bi, i: (0, 0, 0)),
            pl.BlockSpec((1, 1, two, tq), lambda bi, i: (bi, i, T_Q // two, 0)),
            pl.BlockSpec((1, s, two), lambda bi, i: (bi, 0, NAT_K // two)),
            pl.BlockSpec((1, nblk, two, tq), lambda bi, i: (bi, 0, T_VA // two, 0)),
        ],
        out_specs=pl.BlockSpec((1, tq, VA_COLS), lambda bi, i: (bi, i, 0)),
        scratch_shapes=[
            pltpu.VMEM((2, STACK, m), BF16),
            pltpu.VMEM((2, 1, m), F32),
            pltpu.VMEM((2, N_DIFF_HEADS, DIFF_V_DIM + ONES_ROWS, tq), F32),
        ],
        compiler_params=_cparams(("arbitrary", "arbitrary")),
        name="diff_attn",
    )(bound, lamv, subln_g, bias, proj_t, nat3, proj_t)


def _score_bound(q_gain, k_gain, bias):
    qk = HEAD_DIM * jnp.max(jnp.abs(q_gain)) * jnp.max(jnp.abs(k_gain)) * BF16_NORM_MARGIN
    finite = bias > 0.5 * NEG_INF
    b_hi = jnp.maximum(jnp.max(jnp.where(finite, bias, 0.0)), 0.0)
    b_lo = jnp.minimum(jnp.min(jnp.where(finite, bias, 0.0)), 0.0)
    bound = qk + b_hi
    safe = (2.0 * qk + (b_hi - b_lo)) <= SAFE_EXP2_SPAN
    return jnp.stack([bound, safe.astype(F32)]).astype(F32)


def _sb_attn_kernel(tri_ref, q_ref, k_ref, v_ref, o_ref, qs_sc, acc_sc, c_sc, *, tq):
    i = pl.program_id(2)
    m = HEADS_PER_STACK * tq
    _stack_heads_t(q_ref[0, 0], qs_sc, tq)
    acc_sc[...] = jnp.zeros(acc_sc.shape, F32)
    c_sc[...] = jnp.zeros(c_sc.shape, F32)

    def step(blocks):
        mask = (lax.broadcasted_iota(I32, (tq, tq), 0)
                < lax.broadcasted_iota(I32, (tq, tq), 1))
        chains = [(j, diag, h) for j, diag in blocks for h in range(HEADS_PER_STACK)]
        tri = tri_ref[...]
        zs = []
        for j, _, h in chains:
            kb = k_ref[0, pl.ds(pl.multiple_of(j * tq, tq), tq), :]
            zs.append(_dot(kb, qs_sc[:, h * tq:(h + 1) * tq]))
        parts = []
        for (j, diag, h), z in zip(chains, zs):
            nz = -z
            log_1m = jnp.minimum(nz, 0.0) - jnp.log(1.0 + jnp.exp(jnp.minimum(z, nz)))
            log_sig = z + log_1m
            if diag:
                log_1m = jnp.where(mask, log_1m, 0.0)
            parts.append((log_sig, log_1m) + _split_bf16(log_1m))
        sufs = [_dot(tri, hi) + _dot(tri, lo) for _, _, hi, lo in parts]
        c = [c_sc[:, h * tq:(h + 1) * tq] for h in range(HEADS_PER_STACK)]
        weights = []
        for (j, diag, h), (log_sig, log_1m, _, _), suf in zip(chains, parts, sufs):
            a = jnp.exp(log_sig + (suf + c[h]))
            if diag:
                a = jnp.where(mask, a, 0.0)
            c[h] = c[h] + jnp.sum(log_1m, axis=0, keepdims=True)
            weights.append(a.astype(BF16))
        for h in range(HEADS_PER_STACK):
            c_sc[:, h * tq:(h + 1) * tq] = c[h]
        for (j, _, h), a in zip(chains, weights):
            rows = slice(h * HEAD_DIM, (h + 1) * HEAD_DIM)
            acc_sc[rows, :] += _dot(v_ref[0, j, rows, :], a)

    @pl.when(i == 0)
    def _():
        step([(i, True)])

    @pl.when(i > 0)
    def _():
        step([(i, True), (i - 1, False)])

    def cond(j):
        return jnp.logical_and(j >= 0, jnp.max(c_sc[...]) > F32_EXP_ZERO_BELOW)

    def body(j):
        step([(j, False)])
        return j - 1

    lax.while_loop(cond, body, i - 2)

    o_ref[0] = jnp.transpose(acc_sc[...]).astype(BF16)


def _sb_attn(nat3, proj_t, tri, *, tq):
    b, s, _ = nat3.shape
    nblk = s // tq
    n_stacks = SB_COLS // STACK
    m = HEADS_PER_STACK * tq
    kern = functools.partial(_sb_attn_kernel, tq=tq)
    return pl.pallas_call(
        kern,
        out_shape=jax.ShapeDtypeStruct((b, s, SB_COLS), BF16),
        grid=(b, n_stacks, nblk),
        in_specs=[
            pl.BlockSpec((tq, tq), lambda bi, g, i: (0, 0)),
            pl.BlockSpec((1, 1, STACK, tq), lambda bi, g, i: (bi, i, T_QS // STACK + g, 0)),
            pl.BlockSpec((1, s, STACK), lambda bi, g, i: (bi, 0, NAT_KS // STACK + g)),
            pl.BlockSpec((1, nblk, STACK, tq), lambda bi, g, i: (bi, 0, T_VS // STACK + g, 0)),
        ],
        out_specs=pl.BlockSpec((1, tq, STACK), lambda bi, g, i: (bi, i, g)),
        scratch_shapes=[pltpu.VMEM((STACK, m), BF16), pltpu.VMEM((STACK, tq), F32),
                        pltpu.VMEM((1, m), F32)],
        compiler_params=_cparams(("arbitrary", "arbitrary", "arbitrary")),
        name="sb_attn",
    )(tri, proj_t, nat3, proj_t)


ROUTER_ROWS = 8 + N_EXPERTS


def _post_attn_kernel(x_ref, oa_ref, ob_ref, ga_ref, gb_ref, wbd_ref, wbs_ref, wo_ref, g_ref,
                      wrh_ref, wrl_ref, tri_ref,
                      x1_ref, xn_ref, route_ref, gates_ref, counts_ref, cnt_sc, *, tm, d_model):
    @pl.when(pl.program_id(0) == 0)
    def _():
        cnt_sc[...] = jnp.zeros(cnt_sc.shape, F32)

    ga = ga_ref[...].astype(F32)
    gb = gb_ref[...].astype(F32)
    mixed = ga * _dot(oa_ref[...], wbd_ref[...]) + gb * _dot(ob_ref[...], wbs_ref[...])
    x1 = x_ref[...] + _dot(mixed.astype(BF16), wo_ref[...])
    x1_ref[...] = x1
    ms = jnp.mean(x1 * x1, axis=-1, keepdims=True)
    xn = (x1 * lax.rsqrt(ms + EPS)) * g_ref[...]
    xn_ref[...] = xn

    xh, xl = _split_bf16(xn)
    wrh = wrh_ref[...]
    logits = _nt_dot(wrh, xh) + _nt_dot(wrh, xl) + _nt_dot(wrl_ref[...], xh)

    gl = [logits[r:r + 1] for r in range(N_GROUPS)]
    gmax = functools.reduce(jnp.maximum, gl)
    grp = jnp.full((1, tm), N_GROUPS - 1, I32)
    for r in range(N_GROUPS - 2, -1, -1):
        grp = jnp.where(gl[r] == gmax, r, grp)
    pg = 1.0 / functools.reduce(lambda a, b: a + b, [jnp.exp(v - gmax) for v in gl])

    el = logits[8:8 + EXPERTS_PER_GROUP]
    for r in range(1, N_GROUPS):
        lo = 8 + r * EXPERTS_PER_GROUP
        el = jnp.where(grp == r, logits[lo:lo + EXPERTS_PER_GROUP], el)
    ex = jnp.exp(el - jnp.max(el, axis=0, keepdims=True))
    prob = ex / jnp.sum(ex, axis=0, keepdims=True)
    sub = lax.broadcasted_iota(I32, (EXPERTS_PER_GROUP, tm), 0).astype(F32)
    none = float(EXPERTS_PER_GROUP)
    v1 = jnp.max(prob, axis=0, keepdims=True)
    i1 = jnp.min(jnp.where(prob == v1, sub, none), axis=0, keepdims=True)
    rest = jnp.where(sub == i1, -1.0, prob)
    v2 = jnp.max(rest, axis=0, keepdims=True)
    i2 = jnp.min(jnp.where(rest == v2, sub, none), axis=0, keepdims=True)
    denom = v1 + v2
    gate1 = pg * v1 / denom
    gate2 = pg * v2 / denom
    e1 = grp * EXPERTS_PER_GROUP + i1.astype(I32)
    e2 = grp * EXPERTS_PER_GROUP + i2.astype(I32)

    eio = lax.broadcasted_iota(I32, (N_EXPERTS, tm), 0)
    hit1 = eio == e1
    hit2 = eio == e2
    onehot = jnp.where(hit1, 1.0, 0.0) + jnp.where(hit2, 1.0, 0.0)
    prefix = _dot(onehot.astype(BF16), tri_ref[...]) + cnt_sc[...]
    r1 = jnp.sum(jnp.where(hit1, prefix, 0.0), axis=0, keepdims=True)
    r2 = jnp.sum(jnp.where(hit2, prefix, 0.0), axis=0, keepdims=True)
    cnt = cnt_sc[...] + jnp.sum(onehot, axis=1, keepdims=True)
    cnt_sc[...] = cnt
    counts_ref[...] = jnp.broadcast_to(cnt, counts_ref.shape).astype(I32)

    row = lax.broadcasted_iota(I32, (8, tm), 0)
    route_ref[...] = jnp.where(row == 0, e1, jnp.where(row == 1, e2, jnp.where(
        row == 2, r1.astype(I32), jnp.where(row == 3, r2.astype(I32), 0))))
    gates_ref[...] = jnp.where(row == 0, gate1, jnp.where(row == 1, gate2, 0.0))


def _post_attn(x2, oa, ob, nat, wbd, wbs, wo, ln_g, wrh, wrl, tri, *, tm):
    t, d_model = x2.shape
    kern = functools.partial(_post_attn_kernel, tm=tm, d_model=d_model)
    gate_blk = NAT_GATE // d_model
    assert gate_blk * d_model == NAT_GATE
    const = lambda i: (0, 0)
    return pl.pallas_call(
        kern,
        out_shape=(
            jax.ShapeDtypeStruct((t, d_model), F32),
            jax.ShapeDtypeStruct((t, d_model), F32),
            jax.ShapeDtypeStruct((8, t), I32),
            jax.ShapeDtypeStruct((8, t), F32),
            jax.ShapeDtypeStruct((N_EXPERTS, V7X_LANES), I32),
        ),
        grid=(t // tm,),
        in_specs=[
            pl.BlockSpec((tm, d_model), lambda i: (i, 0)),
            pl.BlockSpec((tm, VA_COLS), lambda i: (i, 0)),
            pl.BlockSpec((tm, SB_COLS), lambda i: (i, 0)),
            pl.BlockSpec((tm, d_model), lambda i: (i, gate_blk)),
            pl.BlockSpec((tm, d_model), lambda i: (i, gate_blk + 1)),
            pl.BlockSpec((VA_COLS, d_model), const),
            pl.BlockSpec((SB_COLS, d_model), const),
            pl.BlockSpec((d_model, d_model), const),
            pl.BlockSpec((1, d_model), const),
            pl.BlockSpec((ROUTER_ROWS, d_model), const),
            pl.BlockSpec((ROUTER_ROWS, d_model), const),
            pl.BlockSpec((tm, tm), const),
        ],
        out_specs=(
            pl.BlockSpec((tm, d_model), lambda i: (i, 0)),
            pl.BlockSpec((tm, d_model), lambda i: (i, 0)),
            pl.BlockSpec((8, tm), lambda i: (0, i)),
            pl.BlockSpec((8, tm), lambda i: (0, i)),
            pl.BlockSpec((N_EXPERTS, V7X_LANES), const),
        ),
        scratch_shapes=[pltpu.VMEM((N_EXPERTS, 1), F32)],
        compiler_params=_cparams(("arbitrary",)),
        name="post_attn",
    )(x2, oa, ob, nat, nat, wbd, wbs, wo, ln_g, wrh, wrl, tri)


ROW_UNROLL = 8


def _row_copy(src, src_row, dst, dst_row, sem):
    return pltpu.make_async_copy(src.at[pl.ds(src_row, 1)], dst.at[pl.ds(dst_row, 1)], sem)


def _dispatch_kernel(lastblk_ref, nblk_ref, dest_ref, xn_ref, xs_ref, zero_sc, xbuf, load_sems,
                     row_sems, zsem, *, tm, tmb, nblk_max):
    @pl.when(pl.program_id(0) == 0)
    def _():
        zero_sc[...] = jnp.zeros(zero_sc.shape, F32)

        def zero_block(row):
            return pltpu.make_async_copy(
                zero_sc, xs_ref.at[pl.ds(pl.multiple_of(row, tmb), tmb)], zsem)

        for go in (lambda c: c.start(), lambda c: c.wait()):
            def seg_block(e, carry, go=go):
                @pl.when(lastblk_ref[e] >= 0)
                def _():
                    go(zero_block(lastblk_ref[e]))
                return carry

            def tail_block(b, carry, go=go):
                go(zero_block(b * tmb))
                return carry

            lax.fori_loop(0, N_EXPERTS, seg_block, 0)
            lax.fori_loop(nblk_ref[0], nblk_max, tail_block, 0)

    i = pl.program_id(0)
    n = pl.num_programs(0)

    def load(tile, slot):
        return pltpu.make_async_copy(xn_ref.at[pl.ds(pl.multiple_of(tile * tm, tm), tm)],
                                     xbuf.at[slot], load_sems.at[slot])

    def wait_rows(slot):
        def wait(r, carry):
            for k in range(2):
                _row_copy(xbuf.at[slot], 0, xs_ref, 0, row_sems.at[slot]).wait()
            return carry

        lax.fori_loop(0, tm, wait, 0, unroll=ROW_UNROLL)

    slot = i % DISPATCH_SLOTS

    @pl.when(i == 0)
    def _():
        load(0, 0).start()

    @pl.when(i + 1 < n)
    def _():
        load(i + 1, (i + 1) % DISPATCH_SLOTS).start()

    load(i, slot).wait()
    for r in range(tm):
        for k in range(2):
            _row_copy(xbuf.at[slot], r, xs_ref, dest_ref[k, r], row_sems.at[slot]).start(priority=k)

    @pl.when(i > 0)
    def _():
        wait_rows((i - 1) % DISPATCH_SLOTS)

    @pl.when(i == n - 1)
    def _():
        wait_rows(slot)


DISPATCH_SLOTS = 3


def _dispatch(lastblk, nblk, dest, xn, *, tmb, nblk_max, tm):
    t, d_model = xn.shape
    kern = functools.partial(_dispatch_kernel, tm=tm, tmb=tmb, nblk_max=nblk_max)
    return pl.pallas_call(
        kern,
        out_shape=jax.ShapeDtypeStruct((nblk_max * tmb, d_model), F32),
        grid_spec=pltpu.PrefetchScalarGridSpec(
            num_scalar_prefetch=2,
            grid=(t // tm,),
            in_specs=[
                pl.BlockSpec((8, tm), lambda i, *_: (0, i), memory_space=pltpu.SMEM),
                pl.BlockSpec(memory_space=pl.ANY),
            ],
            out_specs=pl.BlockSpec(memory_space=pl.ANY),
            scratch_shapes=[pltpu.VMEM((tmb, d_model), F32),
                            pltpu.VMEM((DISPATCH_SLOTS, tm, d_model), F32),
                            pltpu.SemaphoreType.DMA((DISPATCH_SLOTS,)),
                            pltpu.SemaphoreType.DMA((DISPATCH_SLOTS,)),
                            pltpu.SemaphoreType.DMA],
        ),
        compiler_params=_cparams(("arbitrary",)),
        name="dispatch",
    )(lastblk, nblk, dest, xn)


def _combine_kernel(dest_ref, dest_next_ref, gates_ref, x1_ref, y_ref, o_ref, buf, sems, *, tm):
    i = pl.program_id(0)
    slot = i % 2

    def issue(idx_ref, to_slot):
        for r in range(tm):
            for k in range(2):
                _row_copy(y_ref, idx_ref[k, r], buf.at[to_slot, k], r, sems.at[to_slot]).start(priority=k)

    @pl.when(i == 0)
    def _():
        issue(dest_ref, 0)

    @pl.when(i + 1 < pl.num_programs(0))
    def _():
        issue(dest_next_ref, 1 - slot)

    def wait(r, carry):
        for k in range(2):
            _row_copy(y_ref, 0, buf.at[slot, k], 0, sems.at[slot]).wait()
        return carry

    lax.fori_loop(0, tm, wait, 0, unroll=ROW_UNROLL)

    gt = jnp.transpose(gates_ref[...])
    o_ref[...] = x1_ref[...] + gt[:, 0:1] * buf[slot, 0] + gt[:, 1:2] * buf[slot, 1]


def _combine(dest, gates, x1, y, *, tm):
    t, d_model = x1.shape
    n = t // tm
    kern = functools.partial(_combine_kernel, tm=tm)
    return pl.pallas_call(
        kern,
        out_shape=jax.ShapeDtypeStruct((t, d_model), F32),
        grid=(n,),
        in_specs=[
            pl.BlockSpec((8, tm), lambda i: (0, i), memory_space=pltpu.SMEM),
            pl.BlockSpec((8, tm), lambda i: (0, jnp.minimum(i + 1, n - 1)), memory_space=pltpu.SMEM),
            pl.BlockSpec((8, tm), lambda i: (0, i)),
            pl.BlockSpec((tm, d_model), lambda i: (i, 0)),
            pl.BlockSpec(memory_space=pl.ANY),
        ],
        out_specs=pl.BlockSpec((tm, d_model), lambda i: (i, 0)),
        scratch_shapes=[pltpu.VMEM((2, 2, tm, d_model), F32), pltpu.SemaphoreType.DMA((2,))],
        compiler_params=_cparams(("arbitrary",)),
        name="combine",
    )(dest, dest, gates, x1, y)


def _experts_kernel(blk_e_ref, nblk_ref, xs_ref, wg_ref, wu_ref, wd_ref, y_ref,
                    wgu_sc, wd_sc, *, d_ff):
    b = pl.program_id(0)
    active = b < nblk_ref[0]

    @pl.when(active)
    def _():
        prev = blk_e_ref[jnp.maximum(b - 1, 0)]

        @pl.when(jnp.logical_or(b == 0, blk_e_ref[b] != prev))
        def _():
            wgu_sc[:, :d_ff] = wg_ref[0, 0].astype(BF16)
            wgu_sc[:, d_ff:] = wu_ref[0, 0].astype(BF16)
            wd_sc[...] = wd_ref[0, 0].astype(BF16)

        gu = _dot(xs_ref[...].astype(BF16), wgu_sc[...])
        gate = gu[:, :d_ff]
        act = (gate * jax.nn.sigmoid(gate)) * gu[:, d_ff:]
        y_ref[...] = _dot(act.astype(BF16), wd_sc[...])

    @pl.when(jnp.logical_not(active))
    def _():
        y_ref[...] = jnp.zeros(y_ref.shape, F32)


def _experts(blk_e, nblk, xs, w_gate, w_up, w_down, *, layer, tmb):
    rows, d_model = xs.shape
    d_ff = w_gate.shape[-1]
    kern = functools.partial(_experts_kernel, d_ff=d_ff)

    def x_map(b, be, nb):
        return (jnp.minimum(b, nb[0] - 1), 0)

    def w_map(b, be, nb):
        return (layer, be[b], 0, 0)

    return pl.pallas_call(
        kern,
        out_shape=jax.ShapeDtypeStruct((rows, d_model), F32),
        grid_spec=pltpu.PrefetchScalarGridSpec(
            num_scalar_prefetch=2,
            grid=(rows // tmb,),
            in_specs=[
                pl.BlockSpec((tmb, d_model), x_map),
                pl.BlockSpec((1, 1, d_model, d_ff), w_map),
                pl.BlockSpec((1, 1, d_model, d_ff), w_map),
                pl.BlockSpec((1, 1, d_ff, d_model), w_map),
            ],
            out_specs=pl.BlockSpec((tmb, d_model), lambda b, be, nb: (b, 0)),
            scratch_shapes=[pltpu.VMEM((d_model, 2 * d_ff), BF16), pltpu.VMEM((d_ff, d_model), BF16)],
        ),
        compiler_params=_cparams(("arbitrary",)),
        name="experts",
    )(blk_e, nblk, xs, w_gate, w_up, w_down)


class _Tiles:
    def __init__(self, t, s):
        self.attn = min(256, s)
        self.proj = min(512, s)
        self.rows = min(256, t)
        self.combine = min(256, t)
        self.expert = min(512, t)


def _strict_upper(n):
    a = jnp.arange(n)
    return (a[:, None] < a[None, :]).astype(BF16)


def _block_table(counts, tmb, nblk_max):
    padded = ((counts + tmb - 1) // tmb) * tmb
    pend = jnp.cumsum(padded)
    pstart = (pend - padded).astype(I32)
    blk_first = jnp.arange(nblk_max, dtype=I32) * tmb
    blk_e = jnp.minimum(jnp.sum(pend[None, :] <= blk_first[:, None], axis=1), N_EXPERTS - 1).astype(I32)
    nblk = (pend[-1] // tmb).astype(I32)
    last_e = jnp.sum(jnp.where(jnp.arange(nblk_max) == nblk - 1, blk_e, 0))
    blk_e = jnp.where(jnp.arange(nblk_max) < nblk, blk_e, last_e)
    lastblk = jnp.where(padded > 0, pend - tmb, -1).astype(I32)
    return pstart, lastblk, blk_e, nblk.reshape(1)


def kernel(x, rel_bias, ln1_g, w_in, qnorm_g, knorm_g, lambda_q1, lambda_k1, lambda_q2, lambda_k2,
           subln_g, w_branch_diff, w_branch_sb, w_out, ln2_g, w_group, w_router, w_gate, w_up, w_down):
    b, s, d_model = x.shape
    depth = w_in.shape[0]
    t = b * s
    tiles = _Tiles(t, s)
    nblk_max = 2 * t // tiles.expert + N_EXPERTS

    bias = _bias_tiles_t(rel_bias, tiles.attn) * LOG2_E
    head = jnp.arange(STACK) // HEAD_DIM
    seg = jnp.where(head[:, None] == head[None, :], 1.0 / HEAD_DIM, 0.0).astype(BF16)
    tri_attn = _strict_upper(tiles.attn)
    tri_tok = _strict_upper(tiles.proj)

    c_k, c_va = 2 * STACK, QK_COLS
    c_qs = c_va + VA_COLS
    c_ks, c_vs, c_g = c_qs + SB_COLS, c_qs + 2 * SB_COLS, c_qs + 3 * SB_COLS

    x2 = x.reshape(t, d_model)
    for l in range(depth):
        lam_init = 0.8 - 0.6 * math.exp(-0.3 * l)
        w = w_in[l].astype(BF16)
        wn = jnp.concatenate([w[:, c_k:c_va], w[:, c_ks:c_vs], w[:, c_g:]], axis=1)
        wt = jnp.transpose(jnp.concatenate(
            [w[:, :c_k], w[:, c_va:c_qs], w[:, c_qs:c_ks], w[:, c_vs:c_g]], axis=1))
        n_heads = 2 * N_DIFF_HEADS
        k_gain = jnp.tile(knorm_g[l].astype(F32), n_heads).reshape(1, 2 * STACK)
        q_gain = jnp.tile(qnorm_g[l].astype(F32) * (HEAD_DIM ** -0.5 * LOG2_E), n_heads)
        bound = _score_bound(q_gain, k_gain, bias)
        q_gain = jnp.broadcast_to(q_gain[:, None], (2 * STACK, tiles.proj))
        lamv = jnp.stack([lambda_q1[l], lambda_k1[l], lambda_q2[l], lambda_k2[l]]).astype(F32)

        nat, proj_t = _in_proj(x2, ln1_g[l].reshape(1, d_model), wn, wt, k_gain, seg, q_gain,
                               batch=b, seq=s, tm=tiles.proj, blk=tiles.attn)
        nat3 = nat.reshape(b, s, nat.shape[-1])
        oa = _diff_attn(nat3, proj_t, bound, lamv, subln_g[l].reshape(1, DIFF_V_DIM), bias,
                        tq=tiles.attn, lam_init=lam_init)
        ob = _sb_attn(nat3, proj_t, tri_attn, tq=tiles.attn)

        w_rt = jnp.concatenate([w_group[l].T, jnp.zeros((8 - N_GROUPS, d_model), F32), w_router[l].T])
        wrh, wrl = _split_bf16(w_rt.astype(F32))
        x1, xn, route, gates, counts = _post_attn(
            x2, oa.reshape(t, VA_COLS), ob.reshape(t, SB_COLS), nat,
            w_branch_diff[l].astype(BF16), w_branch_sb[l].astype(BF16), w_out[l].astype(BF16),
            ln2_g[l].reshape(1, d_model), wrh, wrl, tri_tok, tm=tiles.proj)

        pstart, lastblk, blk_e, nblk = _block_table(counts[:, 0], tiles.expert, nblk_max)
        seg_start = jnp.sum(jnp.where(route[:2, :, None] == jnp.arange(N_EXPERTS), pstart, 0), axis=-1)
        dest = jnp.concatenate([seg_start + route[2:4], jnp.zeros((6, t), I32)])
        xs = _dispatch(lastblk, nblk, dest, xn, tmb=tiles.expert, nblk_max=nblk_max, tm=tiles.rows)
        y = _experts(blk_e, nblk, xs, w_gate, w_up, w_down, layer=l, tmb=tiles.expert)
        x2 = _combine(dest, gates, x1, y, tm=tiles.combine)
    return x2.reshape(b, s, d_model)
```
